```python
import jax, jax.numpy as jnp
from jax import lax
import numpy as np

D_MODEL = 1024
BATCH = 8
SEQ = 2048
DEPTH = 4
DEC_BATCH = 128
DEC_SEQ = 4
PAST_LEN = 16384
PAGE_SIZE = 128

MIX_WIDTH = D_MODEL
H_GLA = 4
DV_GLA = (MIX_WIDTH // 2) // H_GLA
DK_GLA = DV_GLA // 2
GATE_RANK = 16
GATE_TEMP = 16.0
H_RET = 4
DK_RET = (MIX_WIDTH // 2) // H_RET
DV_RET = DK_RET
D_FF = ((8 * D_MODEL // 3 + 127) // 128) * 128
CHUNK = 64
ROPE_BASE = 10000.0
EPS = 1e-6
PROJ_SIZES = (H_GLA * DK_GLA, H_GLA * DK_GLA, H_GLA * DV_GLA, H_GLA * DV_GLA, GATE_RANK,
              H_RET * DK_RET, H_RET * DK_RET, H_RET * DV_RET, H_RET * DV_RET)
PROJ_COLS = sum(PROJ_SIZES)

kernel_name = "hymba_gla_retnet_macaron_step"


def rmsnorm(x, g):
    xf = x.astype(jnp.float32)
    y = xf * lax.rsqrt(jnp.mean(xf * xf, axis=-1, keepdims=True) + EPS)
    return (y * g.astype(jnp.float32)).astype(x.dtype)


def swiglu(x, w_gu, w_down):
    gate, up = jnp.split(x @ w_gu, 2, axis=-1)
    return (jax.nn.silu(gate) * up) @ w_down


def head_rmsnorm(o, g, dtype):
    B, T, H, dv = o.shape
    y = o * lax.rsqrt(jnp.mean(o * o, axis=-1, keepdims=True) + EPS)
    return (y.reshape(B, T, H * dv) * g.astype(jnp.float32)).astype(dtype)


def rotary(x, pos):
    half = x.shape[-1] // 2
    inv_freq = ROPE_BASE ** (-jnp.arange(half, dtype=jnp.float32) / half)
    ang = pos[:, None] * inv_freq[None, :]
    cos = jnp.cos(ang)[None, :, None, :]
    sin = jnp.sin(ang)[None, :, None, :]
    x1, x2 = x[..., :half], x[..., half:]
    return jnp.concatenate([x1 * cos - x2 * sin, x1 * sin + x2 * cos], axis=-1)


def to_chunks(x):
    B, T, H, d = x.shape
    n = -(-T // CHUNK)
    x = jnp.pad(x, ((0, 0), (0, n * CHUNK - T), (0, 0), (0, 0)))
    return x.reshape(B, n, CHUNK, H, d).transpose(1, 0, 3, 2, 4)


def from_chunks(o, T):
    n, B, H, C, d = o.shape
    return o.transpose(1, 0, 3, 2, 4).reshape(B, n * C, H, d)[:, :T]


def causal_mask():
    return jnp.tril(jnp.ones((CHUNK, CHUNK), dtype=bool))


def gla_chunk(S, inp):
    q, k, v, lg = inp
    b = jnp.cumsum(lg, axis=2)
    o_inter = jnp.einsum('bhtk,bhkv->bhtv', q * jnp.exp(b), S)
    diff = b[:, :, :, None, :] - b[:, :, None, :, :]
    decay = jnp.exp(jnp.where(causal_mask()[:, :, None], diff, -jnp.inf))
    A = jnp.einsum('bhtk,bhsk,bhtsk->bhts', q, k, decay)
    o = o_inter + jnp.einsum('bhts,bhsv->bhtv', A, v)
    b_last = b[:, :, -1:, :]
    S_new = jnp.exp(b_last[:, :, 0, :])[..., None] * S + jnp.einsum('bhsk,bhsv->bhkv', k * jnp.exp(b_last - b), v)
    return S_new, o


def ret_chunk(S, inp):
    q, k, v, lg = inp
    b = jnp.cumsum(lg[..., 0], axis=2)
    o_inter = jnp.exp(b)[..., None] * jnp.einsum('bhtk,bhkv->bhtv', q, S)
    D = jnp.exp(jnp.where(causal_mask(), b[:, :, :, None] - b[:, :, None, :], -jnp.inf))
    A = jnp.einsum('bhtk,bhsk->bhts', q, k) * D
    o = o_inter + jnp.einsum('bhts,bhsv->bhtv', A, v)
    b_last = b[:, :, -1:]
    S_new = jnp.exp(b_last[:, :, 0])[..., None, None] * S + jnp.einsum('bhsk,bhsv->bhkv', k * jnp.exp(b_last - b)[..., None], v)
    return S_new, o


def chunked_scan(step, S0, q, k, v, lg):
    T = q.shape[1]
    xs = tuple(to_chunks(a.astype(jnp.float32)) for a in (q, k, v, lg))
    S, o = lax.scan(step, S0.astype(jnp.float32), xs)
    return from_chunks(o, T), S


def mixer(h, pos, w_in, w_a2, b_a, gn_gla, gn_ret, w_out, s_gla, s_ret):
    B, T, _ = h.shape
    proj = h @ w_in
    offs, acc = [], 0
    for s in PROJ_SIZES[:-1]:
        acc += s
        offs.append(acc)
    q_a, k_a, v_a, g_a, a_lo, q_r, k_r, v_r, g_r = jnp.split(proj, offs, axis=-1)
    qa = q_a.reshape(B, T, H_GLA, DK_GLA) * (DK_GLA ** -0.5)
    ka = k_a.reshape(B, T, H_GLA, DK_GLA)
    va = v_a.reshape(B, T, H_GLA, DV_GLA)
    lg_a = jax.nn.log_sigmoid((a_lo @ w_a2 + b_a).astype(jnp.float32)).reshape(B, T, H_GLA, DK_GLA) / GATE_TEMP
    o_a, S_a = chunked_scan(gla_chunk, s_gla, qa, ka, va, lg_a)
    o_a = head_rmsnorm(o_a, gn_gla, h.dtype) * jax.nn.silu(g_a)
    qr = rotary(q_r.reshape(B, T, H_RET, DK_RET).astype(jnp.float32), pos)
    kr = rotary(k_r.reshape(B, T, H_RET, DK_RET).astype(jnp.float32), pos) * (DK_RET ** -0.5)
    vr = v_r.reshape(B, T, H_RET, DV_RET)
    log_gamma = jnp.log1p(-jnp.exp2(-5.0 - jnp.arange(H_RET, dtype=jnp.float32)))
    lg_r = jnp.broadcast_to(log_gamma[None, None, :, None], (B, T, H_RET, 1))
    o_r, S_r = chunked_scan(ret_chunk, s_ret, qr, kr, vr, lg_r)
    o_r = head_rmsnorm(o_r, gn_ret, h.dtype) * jax.nn.silu(g_r)
    out = jnp.concatenate([o_a, o_r], axis=-1) @ w_out
    return out, S_a.astype(s_gla.dtype), S_r.astype(s_ret.dtype)


def trunk(x, pos, s_gla, s_ret, norm_g, w1_gu, w1_down, w2_gu, w2_down,
          w_in, w_a2, b_a, gn_gla, gn_ret, w_out):
    new_gla, new_ret = [], []
    for l in range(DEPTH):
        n = norm_g[l]
        x = x + 0.5 * rmsnorm(swiglu(rmsnorm(x, n[0]), w1_gu[l], w1_down[l]), n[1])
        m, S_a, S_r = mixer(rmsnorm(x, n[2]), pos, w_in[l], w_a2[l], b_a[l], gn_gla[l], gn_ret[l], w_out[l],
                            s_gla[l], s_ret[l])
        x = x + rmsnorm(m, n[3])
        x = x + 0.5 * rmsnorm(swiglu(rmsnorm(x, n[4]), w2_gu[l], w2_down[l]), n[5])
        new_gla.append(S_a)
        new_ret.append(S_r)
    return x, jnp.stack(new_gla), jnp.stack(new_ret)


def setup_inputs(seed: int = 0) -> dict:
    key = jax.random.key(seed)
    ks = jax.random.split(key, 16)
    f = jnp.float32
    nrm = lambda k, shape, s: jax.random.normal(k, shape, f) * s
    return {
        "x_prompt": nrm(ks[0], (BATCH, SEQ, D_MODEL), 1.0),
        "x_sample": nrm(ks[1], (DEC_BATCH, DEC_SEQ, D_MODEL), 1.0),
        "state_gla": nrm(ks[2], (DEPTH, DEC_BATCH, H_GLA, DK_GLA, DV_GLA), 2.0),
        "state_ret": nrm(ks[3], (DEPTH, DEC_BATCH, H_RET, DK_RET, DV_RET), 2.0),
        "norm_g": 1.0 + nrm(ks[4], (DEPTH, 6, D_MODEL), 0.05),
        "w1_gu": nrm(ks[5], (DEPTH, D_MODEL, 2 * D_FF), D_MODEL ** -0.5),
        "w1_down": nrm(ks[6], (DEPTH, D_FF, D_MODEL), D_FF ** -0.5),
        "w2_gu": nrm(ks[7], (DEPTH, D_MODEL, 2 * D_FF), D_MODEL ** -0.5),
        "w2_down": nrm(ks[8], (DEPTH, D_FF, D_MODEL), D_FF ** -0.5),
        "w_in": nrm(ks[9], (DEPTH, D_MODEL, PROJ_COLS), D_MODEL ** -0.5),
        "w_a2": nrm(ks[10], (DEPTH, GATE_RANK, H_GLA * DK_GLA), GATE_RANK ** -0.5),
        "b_a": nrm(ks[11], (DEPTH, H_GLA * DK_GLA), 0.1),
        "gn_gla": 1.0 + nrm(ks[12], (DEPTH, H_GLA * DV_GLA), 0.05),
        "gn_ret": 1.0 + nrm(ks[13], (DEPTH, H_RET * DV_RET), 0.05),
        "w_out": nrm(ks[14], (DEPTH, MIX_WIDTH, D_MODEL), MIX_WIDTH ** -0.5),
    }


def reference(x_prompt, x_sample, state_gla, state_ret, norm_g, w1_gu, w1_down, w2_gu, w2_down,
              w_in, w_a2, b_a, gn_gla, gn_ret, w_out):
    params = (norm_g, w1_gu, w1_down, w2_gu, w2_down, w_in, w_a2, b_a, gn_gla, gn_ret, w_out)
    B, T, _ = x_prompt.shape
    zero_gla = jnp.zeros((DEPTH, B, H_GLA, DK_GLA, DV_GLA), x_prompt.dtype)
    zero_ret = jnp.zeros((DEPTH, B, H_RET, DK_RET, DV_RET), x_prompt.dtype)
    pos_p = jnp.arange(T, dtype=jnp.float32)
    y_prompt, gla_p, ret_p = trunk(x_prompt, pos_p, zero_gla, zero_ret, *params)
    Ts = x_sample.shape[1]
    pos_s = PAST_LEN + jnp.arange(Ts, dtype=jnp.float32)
    y_sample, gla_s, ret_s = trunk(x_sample, pos_s, state_gla, state_ret, *params)
    return (y_prompt, y_sample, gla_p, ret_p, gla_s, ret_s)
```

```python
import functools
import math

import jax
import jax.numpy as jnp
from jax import lax
from jax.experimental import pallas as pl
from jax.experimental.pallas import tpu as pltpu

F32 = jnp.float32
BF16 = jnp.bfloat16

D_MODEL = 1024
DEPTH = 4
PAST_LEN = 16384
N_HEADS = 4
DK_GLA = 64
DV = 128
GATE_RANK = 16
GATE_TEMP = 16.0
D_FF = 2816
ROPE_BASE = 10000.0
EPS = 1e-6
GLA_CHUNK = 64

QA, KA, VA, GA, QR, KR, VR, GR, AL = 0, 256, 512, 1024, 1536, 2048, 2560, 3072, 3584
AL_PAD = 128
PROJ_PACKED = AL + AL_PAD

FFN_TOKEN_TILE = 512
FFN_FF_TILE = 256
MIX_TIME_TILE = 256
SAMPLE_SEQ_BLOCK = 16
VMEM_LIMIT_BYTES = 56 * 1024 * 1024

LOG_GAMMA = tuple(math.log1p(-(2.0 ** (-5.0 - h))) for h in range(N_HEADS))


def _mm(a, b):
    return jnp.dot(a, b, preferred_element_type=F32)


def _mm_nt(a, b):
    return lax.dot_general(a, b, (((1,), (1,)), ((), ())), preferred_element_type=F32)


def _mm_tn(a, b):
    return lax.dot_general(a, b, (((0,), (0,)), ((), ())), preferred_element_type=F32)


def _rms(x, g):
    ms = jnp.mean(x * x, axis=-1, keepdims=True)
    return x * lax.rsqrt(ms + EPS) * g


def _silu(x):
    return x * (1.0 / (1.0 + jnp.exp(-x)))


def _log_sigmoid(x):
    return jnp.minimum(x, 0.0) - jnp.log1p(jnp.exp(-jnp.abs(x)))


def _split_bf16(x):
    hi = x.astype(BF16)
    lo = (x - hi.astype(F32)).astype(BF16)
    return hi, lo


def _iota(shape, dim):
    return lax.broadcasted_iota(jnp.int32, shape, dim)


def _rotary(x, cos2, sin2):
    return x * cos2 + pltpu.roll(x, DV // 2, axis=1) * sin2


def _head_norm_gate(o, gn, gate):
    outs = []
    for h in range(N_HEADS):
        oh = o[:, h * DV:(h + 1) * DV]
        ms = jnp.mean(oh * oh, axis=-1, keepdims=True)
        outs.append(oh * lax.rsqrt(ms + EPS))
    return jnp.concatenate(outs, axis=1) * gn * _silu(gate)


def _ffn_kernel(x_ref, ng_ref, wgu_ref, wdn_ref, o_ref, *, pre_row):
    x = x_ref[...]
    xn = _rms(x, ng_ref[pre_row:pre_row + 1, :]).astype(BF16)
    acc = jnp.zeros(x.shape, F32)
    for c in range(D_FF // FFN_FF_TILE):
        lo = c * FFN_FF_TILE
        gate = _mm(xn, wgu_ref[:, lo:lo + FFN_FF_TILE])
        up = _mm(xn, wgu_ref[:, D_FF + lo:D_FF + lo + FFN_FF_TILE])
        act = (_silu(gate) * up).astype(BF16)
        acc = acc + _mm(act, wdn_ref[lo:lo + FFN_FF_TILE, :])
    o_ref[...] = x + 0.5 * _rms(acc, ng_ref[pre_row + 1:pre_row + 2, :])


def _ffn(x_all, ng, wgu, wdn, pre_row, tm):
    n_tok = x_all.shape[0]
    const = lambda i: (0, 0)
    return pl.pallas_call(
        functools.partial(_ffn_kernel, pre_row=pre_row),
        out_shape=jax.ShapeDtypeStruct(x_all.shape, F32),
        grid=(n_tok // tm,),
        in_specs=[
            pl.BlockSpec((tm, D_MODEL), lambda i: (i, 0)),
            pl.BlockSpec(ng.shape, const),
            pl.BlockSpec(wgu.shape, const),
            pl.BlockSpec(wdn.shape, const),
        ],
        out_specs=pl.BlockSpec((tm, D_MODEL), lambda i: (i, 0)),
        compiler_params=pltpu.CompilerParams(
            dimension_semantics=("arbitrary",), vmem_limit_bytes=VMEM_LIMIT_BYTES),
        name="ffn",
    )(x_all, ng, wgu, wdn)


def _mixer_prompt_kernel(x_ref, ng_ref, win_ref, wa2_ref, ba_ref, gn_ref, wout_ref,
                         cos_ref, sin_ref, o_ref, sg_ref, sr_ref, sgt_s, sr_s):
    tt = x_ref.shape[0]
    t_idx = pl.program_id(1)

    @pl.when(t_idx == 0)
    def _():
        sgt_s[...] = jnp.zeros(sgt_s.shape, F32)
        sr_s[...] = jnp.zeros(sr_s.shape, F32)

    x = x_ref[...]
    xn = _rms(x, ng_ref[2:3, :]).astype(BF16)

    def proj(off, width):
        return _mm(xn, win_ref[:, off:off + width])

    q = proj(QA, 256) * (DK_GLA ** -0.5)
    k = proj(KA, 256)
    v = proj(VA, 512).astype(BF16)
    alo = proj(AL, AL_PAD).astype(BF16)
    lg = _log_sigmoid(_mm(alo, wa2_ref[...]) + ba_ref[...]) * (1.0 / GATE_TEMP)

    r = _iota((tt, tt), 0)
    c = _iota((tt, tt), 1)
    tri = jnp.where((jnp.right_shift(r, 6) == jnp.right_shift(c, 6)) & (c <= r), 1.0, 0.0).astype(BF16)
    lg_hi, lg_lo = _split_bf16(lg)
    bcum = _mm(tri, lg_hi) + _mm(tri, lg_lo)

    rows4 = N_HEADS * GLA_CHUNK
    head_mask = jnp.where(jnp.right_shift(_iota((rows4, 256), 0), 6)
                          == jnp.right_shift(_iota((rows4, 256), 1), 6), 1.0, 0.0)
    causal = jnp.bitwise_and(_iota((rows4, GLA_CHUNK), 0), GLA_CHUNK - 1) >= _iota((rows4, GLA_CHUNK), 1)
    bd_mask = jnp.where(jnp.right_shift(_iota((512, 256), 0), 7)
                        == jnp.right_shift(_iota((512, 256), 1), 6), 1.0, 0.0)

    sgt = sgt_s[...]
    o_a_chunks = []
    for ci in range(tt // GLA_CHUNK):
        rs = slice(ci * GLA_CHUNK, (ci + 1) * GLA_CHUNK)
        bc = bcum[rs]
        b_last = bc[GLA_CHUNK - 1:GLA_CHUNK, :]
        qt = q[rs] * jnp.exp(bc)
        kt = (k[rs] * jnp.exp(-bc)).astype(BF16)
        kp = (k[rs] * jnp.exp(b_last - bc)).astype(BF16)
        vc = v[rs]
        qstack = (jnp.concatenate([qt] * N_HEADS, axis=0) * head_mask).astype(BF16)
        a = jnp.where(causal, _mm_nt(qstack, kt), 0.0).astype(BF16)
        rr = _mm(a, vc)
        o_intra = jnp.concatenate(
            [rr[h * GLA_CHUNK:(h + 1) * GLA_CHUNK, h * DV:(h + 1) * DV] for h in range(N_HEADS)], axis=1)
        o_inter = _mm_nt(qt.astype(BF16), sgt.astype(BF16))
        o_a_chunks.append(o_inter + o_intra)
        sgt = jnp.exp(b_last) * sgt + bd_mask * _mm_tn(vc, kp)
    sgt_s[...] = sgt
    o_a = jnp.concatenate(o_a_chunks, axis=0)

    cos2 = cos_ref[...]
    sin2 = sin_ref[...]
    qr = proj(QR, 512)
    kr = proj(KR, 512)
    vr = proj(VR, 512).astype(BF16)
    row = _iota((tt, DV), 0).astype(F32)
    lag = (r - c).astype(F32)
    o_r_heads = []
    for h in range(N_HEADS):
        hs = slice(h * DV, (h + 1) * DV)
        lgam = LOG_GAMMA[h]
        qh = _rotary(qr[:, hs], cos2, sin2).astype(BF16)
        kh = _rotary(kr[:, hs], cos2, sin2) * (DV ** -0.5)
        vh = vr[:, hs]
        dmat = jnp.where(lag >= 0.0, jnp.exp(lag * lgam), 0.0)
        a = (_mm_nt(qh, kh.astype(BF16)) * dmat).astype(BF16)
        s_old = sr_s[h]
        o_r_heads.append(jnp.exp((row + 1.0) * lgam) * _mm(qh, s_old.astype(BF16)) + _mm(a, vh))
        kd = (kh * jnp.exp((tt - 1.0 - row) * lgam)).astype(BF16)
        sr_s[h] = math.exp(tt * lgam) * s_old + _mm_tn(kd, vh)
    o_r = jnp.concatenate(o_r_heads, axis=1)

    y_a = _head_norm_gate(o_a, gn_ref[:, 0:512], proj(GA, 512))
    y_r = _head_norm_gate(o_r, gn_ref[:, 512:1024], proj(GR, 512))
    y = jnp.concatenate([y_a, y_r], axis=1).astype(BF16)
    o_ref[...] = x + _rms(_mm(y, wout_ref[...]), ng_ref[3:4, :])

    @pl.when(t_idx == pl.num_programs(1) - 1)
    def _():
        sg = sgt_s[...].T
        for h in range(N_HEADS):
            sg_ref[0, h] = sg[h * DK_GLA:(h + 1) * DK_GLA, h * DV:(h + 1) * DV]
        sr_ref[0] = sr_s[...]


def _mixer_prompt(x_all, batch, seq, ng, win, wa2, ba, gn, wout, cos2, sin2):
    tt = MIX_TIME_TILE
    n_t = seq // tt
    const = lambda b, t: (0, 0)
    x_map = lambda b, t: (b * n_t + t, 0)
    out_shapes = (
        jax.ShapeDtypeStruct(x_all.shape, F32),
        jax.ShapeDtypeStruct((batch, N_HEADS, DK_GLA, DV), F32),
        jax.ShapeDtypeStruct((batch, N_HEADS, DV, DV), F32),
    )
    return pl.pallas_call(
        _mixer_prompt_kernel,
        out_shape=out_shapes,
        grid=(batch, n_t),
        in_specs=[
            pl.BlockSpec((tt, D_MODEL), x_map),
            pl.BlockSpec(ng.shape, const),
            pl.BlockSpec(win.shape, const),
            pl.BlockSpec(wa2.shape, const),
            pl.BlockSpec(ba.shape, const),
            pl.BlockSpec(gn.shape, const),
            pl.BlockSpec(wout.shape, const),
            pl.BlockSpec((tt, DV), lambda b, t: (t, 0)),
            pl.BlockSpec((tt, DV), lambda b, t: (t, 0)),
        ],
        out_specs=(
            pl.BlockSpec((tt, D_MODEL), x_map),
            pl.BlockSpec((1, N_HEADS, DK_GLA, DV), lambda b, t: (b, 0, 0, 0)),
            pl.BlockSpec((1, N_HEADS, DV, DV), lambda b, t: (b, 0, 0, 0)),
        ),
        scratch_shapes=[
            pltpu.VMEM((N_HEADS * DV, N_HEADS * DK_GLA), F32),
            pltpu.VMEM((N_HEADS, DV, DV), F32),
        ],
        input_output_aliases={0: 0},
        compiler_params=pltpu.CompilerParams(
            dimension_semantics=("arbitrary", "arbitrary"), vmem_limit_bytes=VMEM_LIMIT_BYTES),
        name="mixer_prompt",
    )(x_all, ng, win, wa2, ba, gn, wout, cos2, sin2)


def _proj_sample_kernel(x_ref, ng_ref, win_ref, wa2_ref, ba_ref, p_ref, lg_ref):
    xn = _rms(x_ref[...], ng_ref[2:3, :]).astype(BF16)
    p = _mm(xn, win_ref[...])
    p_ref[...] = p
    alo = p[:, AL:AL + AL_PAD].astype(BF16)
    lg_ref[...] = _log_sigmoid(_mm(alo, wa2_ref[...]) + ba_ref[...]) * (1.0 / GATE_TEMP)


def _proj_sample(x_all, row_block, n_rows, ng, win, wa2, ba):
    const = lambda i: (0, 0)
    return pl.pallas_call(
        _proj_sample_kernel,
        out_shape=(jax.ShapeDtypeStruct((n_rows, PROJ_PACKED), F32),
                   jax.ShapeDtypeStruct((n_rows, 256), F32)),
        grid=(1,),
        in_specs=[
            pl.BlockSpec((n_rows, D_MODEL), lambda i: (row_block, 0)),
            pl.BlockSpec(ng.shape, const),
            pl.BlockSpec(win.shape, const),
            pl.BlockSpec(wa2.shape, const),
            pl.BlockSpec(ba.shape, const),
        ],
        out_specs=(pl.BlockSpec((n_rows, PROJ_PACKED), const),
                   pl.BlockSpec((n_rows, 256), const)),
        compiler_params=pltpu.CompilerParams(
            dimension_semantics=("arbitrary",), vmem_limit_bytes=VMEM_LIMIT_BYTES),
        name="proj_sample",
    )(x_all, ng, win, wa2, ba)


def _core_sample_kernel(p_ref, lg_ref, cos_ref, sin_ref, sg_in, sr_in, o_ref, sg_out, sr_out):
    n_seq, n_tok = p_ref.shape[0], p_ref.shape[1]
    rows4 = N_HEADS * n_tok
    tok_shift = n_tok.bit_length() - 1
    head_mask = jnp.where(jnp.right_shift(_iota((rows4, 256), 0), tok_shift)
                          == jnp.right_shift(_iota((rows4, 256), 1), 6), 1.0, 0.0)
    causal = jnp.bitwise_and(_iota((rows4, n_tok), 0), n_tok - 1) >= _iota((rows4, n_tok), 1)
    ones = jnp.ones((2 * n_tok, DV), BF16)
    cos2 = cos_ref[...]
    sin2 = sin_ref[...]
    row = _iota((n_tok, DV), 0).astype(F32)
    lag = (_iota((n_tok, n_tok), 0) - _iota((n_tok, n_tok), 1)).astype(F32)

    def body(n, carry):
        p = p_ref[n]
        lg = lg_ref[n]
        b_rows = [lg[0:1]]
        for t in range(1, n_tok):
            b_rows.append(b_rows[-1] + lg[t:t + 1])
        bc = jnp.concatenate(b_rows, axis=0)
        b_last = b_rows[-1]
        q = p[:, QA:QA + 256] * (DK_GLA ** -0.5)
        k = p[:, KA:KA + 256]
        v = p[:, VA:VA + 512].astype(BF16)
        qt = q * jnp.exp(bc)
        kt = (k * jnp.exp(-bc)).astype(BF16)
        kp = (k * jnp.exp(b_last - bc)).astype(BF16)
        s_old = sg_in[n].reshape(N_HEADS * DK_GLA, DV)
        qstack = (jnp.concatenate([qt] * N_HEADS, axis=0) * head_mask).astype(BF16)
        a = jnp.where(causal, _mm_nt(qstack, kt), 0.0).astype(BF16)
        oo = _mm(qstack, s_old.astype(BF16))
        rr = _mm(a, v)
        o_a = jnp.concatenate(
            [oo[h * n_tok:(h + 1) * n_tok] + rr[h * n_tok:(h + 1) * n_tok, h * DV:(h + 1) * DV]
             for h in range(N_HEADS)], axis=1)
        lg_hi, lg_lo = _split_bf16(lg)
        b_last_t = _mm_tn(jnp.concatenate([lg_hi, lg_lo], axis=0), ones)
        u = _mm_tn(kp, v)
        u_d = jnp.concatenate(
            [u[h * DK_GLA:(h + 1) * DK_GLA, h * DV:(h + 1) * DV] for h in range(N_HEADS)], axis=0)
        sg_out[n] = (jnp.exp(b_last_t) * s_old + u_d).reshape(N_HEADS, DK_GLA, DV)
        vr = p[:, VR:VR + 512].astype(BF16)
        o_r_heads = []
        for h in range(N_HEADS):
            hs = slice(h * DV, (h + 1) * DV)
            lgam = LOG_GAMMA[h]
            qh = _rotary(p[:, QR + h * DV:QR + (h + 1) * DV], cos2, sin2).astype(BF16)
            kh = _rotary(p[:, KR + h * DV:KR + (h + 1) * DV], cos2, sin2) * (DV ** -0.5)
            vh = vr[:, hs]
            dmat = jnp.where(lag >= 0.0, jnp.exp(lag * lgam), 0.0)
            ar = (_mm_nt(qh, kh.astype(BF16)) * dmat).astype(BF16)
            s_r = sr_in[n, h]
            o_r_heads.append(jnp.exp((row + 1.0) * lgam) * _mm(qh, s_r.astype(BF16)) + _mm(ar, vh))
            kd = (kh * jnp.exp((n_tok - 1.0 - row) * lgam)).astype(BF16)
            sr_out[n, h] = math.exp(n_tok * lgam) * s_r + _mm_tn(kd, vh)
        o_ref[n] = jnp.concatenate([o_a] + o_r_heads, axis=1)
        return carry

    lax.fori_loop(0, n_seq, body, 0)


def _core_sample(p3, lg3, cos2, sin2, sg, sr):
    n_seq, n_tok = p3.shape[0], p3.shape[1]
    nb = SAMPLE_SEQ_BLOCK
    const = lambda i: (0, 0)
    out_shapes = (
        jax.ShapeDtypeStruct((n_seq, n_tok, D_MODEL), F32),
        jax.ShapeDtypeStruct(sg.shape, F32),
        jax.ShapeDtypeStruct(sr.shape, F32),
    )
    return pl.pallas_call(
        _core_sample_kernel,
        out_shape=out_shapes,
        grid=(n_seq // nb,),
        in_specs=[
            pl.BlockSpec((nb, n_tok, PROJ_PACKED), lambda i: (i, 0, 0)),
            pl.BlockSpec((nb, n_tok, 256), lambda i: (i, 0, 0)),
            pl.BlockSpec(cos2.shape, const),
            pl.BlockSpec(sin2.shape, const),
            pl.BlockSpec((nb, N_HEADS, DK_GLA, DV), lambda i: (i, 0, 0, 0)),
            pl.BlockSpec((nb, N_HEADS, DV, DV), lambda i: (i, 0, 0, 0)),
        ],
        out_specs=(
            pl.BlockSpec((nb, n_tok, D_MODEL), lambda i: (i, 0, 0)),
            pl.BlockSpec((nb, N_HEADS, DK_GLA, DV), lambda i: (i, 0, 0, 0)),
            pl.BlockSpec((nb, N_HEADS, DV, DV), lambda i: (i, 0, 0, 0)),
        ),
        compiler_params=pltpu.CompilerParams(
            dimension_semantics=("arbitrary",), vmem_limit_bytes=VMEM_LIMIT_BYTES),
        name="core_sample",
    )(p3, lg3, cos2, sin2, sg, sr)


def _out_sample_kernel(o_ref, ga_ref, gr_ref, gn_ref, wout_ref, ng_ref, x_ref, y_ref):
    o = o_ref[...]
    y_a = _head_norm_gate(o[:, 0:512], gn_ref[:, 0:512], ga_ref[...])
    y_r = _head_norm_gate(o[:, 512:1024], gn_ref[:, 512:1024], gr_ref[...])
    y = jnp.concatenate([y_a, y_r], axis=1).astype(BF16)
    y_ref[...] = x_ref[...] + _rms(_mm(y, wout_ref[...]), ng_ref[3:4, :])


def _out_sample(o2, p2, gn, wout, ng, x_all, row_block):
    n_rows = o2.shape[0]
    const = lambda i: (0, 0)
    return pl.pallas_call(
        _out_sample_kernel,
        out_shape=jax.ShapeDtypeStruct(x_all.shape, F32),
        grid=(1,),
        in_specs=[
            pl.BlockSpec((n_rows, D_MODEL), const),
            pl.BlockSpec((n_rows, 512), lambda i: (0, GA // 512)),
            pl.BlockSpec((n_rows, 512), lambda i: (0, GR // 512)),
            pl.BlockSpec(gn.shape, const),
            pl.BlockSpec(wout.shape, const),
            pl.BlockSpec(ng.shape, const),
            pl.BlockSpec((n_rows, D_MODEL), lambda i: (row_block, 0)),
        ],
        out_specs=pl.BlockSpec((n_rows, D_MODEL), lambda i: (row_block, 0)),
        input_output_aliases={6: 0},
        compiler_params=pltpu.CompilerParams(
            dimension_semantics=("arbitrary",), vmem_limit_bytes=VMEM_LIMIT_BYTES),
        name="out_sample",
    )(o2, p2, p2, gn, wout, ng, x_all)


def _rope_tables(pos):
    half = DV // 2
    inv_freq = ROPE_BASE ** (-jnp.arange(half, dtype=F32) / half)
    ang = pos[:, None] * inv_freq[None, :]
    cos, sin = jnp.cos(ang), jnp.sin(ang)
    return jnp.concatenate([cos, cos], axis=-1), jnp.concatenate([-sin, sin], axis=-1)


def _pack_w_in(w_in):
    a0 = GA + 512
    pad = jnp.zeros(w_in.shape[:2] + (AL_PAD - GATE_RANK,), w_in.dtype)
    return jnp.concatenate([w_in[..., :a0], w_in[..., a0 + GATE_RANK:], w_in[..., a0:a0 + GATE_RANK], pad],
                           axis=-1)


def kernel(x_prompt, x_sample, state_gla, state_ret, norm_g, w1_gu, w1_down, w2_gu, w2_down,
           w_in, w_a2, b_a, gn_gla, gn_ret, w_out):
    batch, seq, _ = x_prompt.shape
    n_seq, n_tok, _ = x_sample.shape
    n_prompt = batch * seq
    n_sample = n_seq * n_tok
    assert seq % MIX_TIME_TILE == 0 and n_prompt % n_sample == 0
    assert n_seq % SAMPLE_SEQ_BLOCK == 0 and n_tok & (n_tok - 1) == 0
    sample_block = n_prompt // n_sample
    ffn_tile = math.gcd(FFN_TOKEN_TILE, n_sample)

    w1_gu_b, w1_dn_b = w1_gu.astype(BF16), w1_down.astype(BF16)
    w2_gu_b, w2_dn_b = w2_gu.astype(BF16), w2_down.astype(BF16)
    w_in_b = _pack_w_in(w_in).astype(BF16)
    w_out_b = w_out.astype(BF16)
    w_a2_b = jnp.concatenate(
        [w_a2, jnp.zeros((DEPTH, AL_PAD - GATE_RANK, w_a2.shape[-1]), w_a2.dtype)], axis=1).astype(BF16)
    b_a2 = b_a[:, None, :]
    gn = jnp.concatenate([gn_gla, gn_ret], axis=-1)[:, None, :]

    cos_p, sin_p = _rope_tables(jnp.arange(seq, dtype=F32))
    cos_s, sin_s = _rope_tables(PAST_LEN + jnp.arange(n_tok, dtype=F32))

    x_all = jnp.concatenate([x_prompt.reshape(n_prompt, D_MODEL), x_sample.reshape(n_sample, D_MODEL)], axis=0)
    gla_p, ret_p, gla_s, ret_s = [], [], [], []
    for l in range(DEPTH):
        ng = norm_g[l]
        x_all = _ffn(x_all, ng, w1_gu_b[l], w1_dn_b[l], 0, ffn_tile)
        p2, lg2 = _proj_sample(x_all, sample_block, n_sample, ng, w_in_b[l], w_a2_b[l], b_a2[l])
        o3, sg_s, sr_s = _core_sample(p2.reshape(n_seq, n_tok, PROJ_PACKED), lg2.reshape(n_seq, n_tok, 256),
                                      cos_s, sin_s, state_gla[l], state_ret[l])
        x_all, sg_p, sr_p = _mixer_prompt(x_all, batch, seq, ng, w_in_b[l], w_a2_b[l], b_a2[l], gn[l],
                                          w_out_b[l], cos_p, sin_p)
        x_all = _out_sample(o3.reshape(n_sample, D_MODEL), p2, gn[l], w_out_b[l], ng, x_all, sample_block)
        x_all = _ffn(x_all, ng, w2_gu_b[l], w2_dn_b[l], 4, ffn_tile)
        gla_p.append(sg_p)
        ret_p.append(sr_p)
        gla_s.append(sg_s)
        ret_s.append(sr_s)

    y_prompt = x_all[:n_prompt].reshape(batch, seq, D_MODEL)
    y_sample = x_all[n_prompt:].reshape(n_seq, n_tok, D_MODEL)
    return (y_prompt, y_sample, jnp.stack(gla_p), jnp.stack(ret_p), jnp.stack(gla_s), jnp.stack(ret_s))
```

```python
import functools
import math

import jax
import jax.numpy as jnp
from jax import lax
from jax.experimental import pallas as pl
from jax.experimental.pallas import tpu as pltpu

F32 = jnp.float32
BF16 = jnp.bfloat16

D_MODEL = 1024
DEPTH = 4
PAST_LEN = 16384
N_HEADS = 4
DK_GLA = 64
DV = 128
GATE_RANK = 16
GATE_TEMP = 16.0
D_FF = 2816
ROPE_BASE = 10000.0
EPS = 1e-6
GLA_CHUNK = 64

A_COLS = 1536
R_COLS = 2048
QA, KA, VA, GA = 0, 256, 512, 1024
QR, KR, VR, GR = 0, 512, 1024, 1536
AL_PAD = 128

FFN_SUB_TILE = 512
FFN_FF_TILE = 256
MIX_SUB_TILE = 256
FFN_SUB_TILES_PER_STEP = 1
MIX_SUB_TILES_PER_STEP = 1
SAMPLE_SEQ_BLOCK = 16
VMEM_LIMIT_BYTES = 56 * 1024 * 1024

LOG_GAMMA = tuple(math.log1p(-(2.0 ** (-5.0 - h))) for h in range(N_HEADS))


def _mm(a, b):
    return jnp.dot(a, b, preferred_element_type=F32)


def _mm_nt(a, b):
    return lax.dot_general(a, b, (((1,), (1,)), ((), ())), preferred_element_type=F32)


def _mm_tn(a, b):
    return lax.dot_general(a, b, (((0,), (0,)), ((), ())), preferred_element_type=F32)


def _rms(x, g):
    ms = jnp.mean(x * x, axis=-1, keepdims=True)
    return x * lax.rsqrt(ms + EPS) * g


def _silu(x):
    return x * (1.0 / (1.0 + jnp.exp(-x)))


def _log_sigmoid(x):
    return jnp.minimum(x, 0.0) - jnp.log1p(jnp.exp(-jnp.abs(x)))


def _split_bf16(x):
    hi = x.astype(BF16)
    lo = (x - hi.astype(F32)).astype(BF16)
    return hi, lo


def _iota(shape, dim):
    return lax.broadcasted_iota(jnp.int32, shape, dim)


def _rotary(x, cos2, sin2):
    return x * cos2 + pltpu.roll(x, DV // 2, axis=1) * sin2


def _head_norm_gate(o, gn, gate):
    outs = []
    for h in range(N_HEADS):
        oh = o[:, h * DV:(h + 1) * DV]
        ms = jnp.mean(oh * oh, axis=-1, keepdims=True)
        outs.append(oh * lax.rsqrt(ms + EPS))
    return jnp.concatenate(outs, axis=1) * gn * _silu(gate)


def _layer_spec(arr, layer, n_grid_axes):
    zeros = (0,) * (arr.ndim - 1)
    if n_grid_axes == 1:
        return pl.BlockSpec((None,) + arr.shape[1:], lambda i: (layer,) + zeros)
    return pl.BlockSpec((None,) + arr.shape[1:], lambda i, j: (layer,) + zeros)


def _params(n_axes):
    return pltpu.CompilerParams(dimension_semantics=("arbitrary",) * n_axes,
                                vmem_limit_bytes=VMEM_LIMIT_BYTES)


def _ffn_rows(x, ng_ref, wgu_ref, wdn_ref, pre_row):
    xn = _rms(x, ng_ref[pre_row:pre_row + 1, :]).astype(BF16)
    acc = jnp.zeros(x.shape, F32)
    for c in range(D_FF // FFN_FF_TILE):
        lo = c * FFN_FF_TILE
        gate = _mm(xn, wgu_ref[:, lo:lo + FFN_FF_TILE])
        up = _mm(xn, wgu_ref[:, D_FF + lo:D_FF + lo + FFN_FF_TILE])
        act = (_silu(gate) * up).astype(BF16)
        acc = acc + _mm(act, wdn_ref[lo:lo + FFN_FF_TILE, :])
    return x + 0.5 * _rms(acc, ng_ref[pre_row + 1:pre_row + 2, :])


def _ffn_kernel(x_ref, ng_ref, wgu_ref, wdn_ref, o_ref, *, pre_row):
    for s in range(x_ref.shape[0] // FFN_SUB_TILE):
        rs = slice(s * FFN_SUB_TILE, (s + 1) * FFN_SUB_TILE)
        o_ref[rs, :] = _ffn_rows(x_ref[rs, :], ng_ref, wgu_ref, wdn_ref, pre_row)


def _ffn_split_in_kernel(xp_ref, xs_ref, ng_ref, wgu_ref, wdn_ref, o_ref, *, pre_row, n_prompt_tiles):
    x = jnp.where(pl.program_id(0) < n_prompt_tiles, xp_ref[...], xs_ref[...])
    o_ref[...] = _ffn_rows(x, ng_ref, wgu_ref, wdn_ref, pre_row)


def _ffn_split_out_kernel(x_ref, ng_ref, wgu_ref, wdn_ref, op_ref, os_ref, *, pre_row, n_prompt_tiles):
    y = _ffn_rows(x_ref[...], ng_ref, wgu_ref, wdn_ref, pre_row)
    is_prompt = pl.program_id(0) < n_prompt_tiles

    @pl.when(is_prompt)
    def _():
        op_ref[...] = y

    @pl.when(jnp.logical_not(is_prompt))
    def _():
        os_ref[...] = y


def _ffn(x_all, layer, norm_g, wgu, wdn, pre_row, tile):
    return pl.pallas_call(
        functools.partial(_ffn_kernel, pre_row=pre_row),
        out_shape=jax.ShapeDtypeStruct(x_all.shape, F32),
        grid=(x_all.shape[0] // tile,),
        in_specs=[pl.BlockSpec((tile, D_MODEL), lambda i: (i, 0)),
                  _layer_spec(norm_g, layer, 1), _layer_spec(wgu, layer, 1), _layer_spec(wdn, layer, 1)],
        out_specs=pl.BlockSpec((tile, D_MODEL), lambda i: (i, 0)),
        compiler_params=_params(1),
        name="ffn",
    )(x_all, norm_g, wgu, wdn)


def _ffn_split_in(xp, xs, layer, norm_g, wgu, wdn, pre_row):
    tm = FFN_SUB_TILE
    n_p, n_s = xp.shape[0] // tm, xs.shape[0] // tm
    return pl.pallas_call(
        functools.partial(_ffn_split_in_kernel, pre_row=pre_row, n_prompt_tiles=n_p),
        out_shape=jax.ShapeDtypeStruct((xp.shape[0] + xs.shape[0], D_MODEL), F32),
        grid=(n_p + n_s,),
        in_specs=[pl.BlockSpec((tm, D_MODEL), lambda i: (jnp.minimum(i, n_p - 1), 0)),
                  pl.BlockSpec((tm, D_MODEL), lambda i: (jnp.maximum(i - n_p, 0), 0)),
                  _layer_spec(norm_g, layer, 1), _layer_spec(wgu, layer, 1), _layer_spec(wdn, layer, 1)],
        out_specs=pl.BlockSpec((tm, D_MODEL), lambda i: (i, 0)),
        compiler_params=_params(1),
        name="ffn_split_in",
    )(xp, xs, norm_g, wgu, wdn)


def _ffn_split_out(x_all, n_prompt, layer, norm_g, wgu, wdn, pre_row):
    tm = FFN_SUB_TILE
    n_p, n_s = n_prompt // tm, (x_all.shape[0] - n_prompt) // tm
    return pl.pallas_call(
        functools.partial(_ffn_split_out_kernel, pre_row=pre_row, n_prompt_tiles=n_p),
        out_shape=(jax.ShapeDtypeStruct((n_prompt, D_MODEL), F32),
                   jax.ShapeDtypeStruct((x_all.shape[0] - n_prompt, D_MODEL), F32)),
        grid=(n_p + n_s,),
        in_specs=[pl.BlockSpec((tm, D_MODEL), lambda i: (i, 0)),
                  _layer_spec(norm_g, layer, 1), _layer_spec(wgu, layer, 1), _layer_spec(wdn, layer, 1)],
        out_specs=(pl.BlockSpec((tm, D_MODEL), lambda i: (jnp.minimum(i, n_p - 1), 0)),
                   pl.BlockSpec((tm, D_MODEL), lambda i: (jnp.maximum(i - n_p, 0), 0))),
        compiler_params=_params(1),
        name="ffn_split_out",
    )(x_all, norm_g, wgu, wdn)


def _mixer_rows(x, cos2, sin2, sgt, sr, consts, ng_ref, wa_ref, wr_ref, wl_ref, wa2_ref, ba_ref,
                gn_ref, wout_ref, gq_ref, gk_ref):
    tri, k_head_mask, v_head_mask, causal, bd_mask, causal_r = consts
    tt = x.shape[0]
    xn = _rms(x, ng_ref[2:3, :]).astype(BF16)

    alo = _mm(xn, wl_ref[...]).astype(BF16)
    gate_pre = _mm(alo, wa2_ref[...])
    q = _mm(xn, wa_ref[:, QA:QA + 256]) * (DK_GLA ** -0.5)
    k = _mm(xn, wa_ref[:, KA:KA + 256])
    v = _mm(xn, wa_ref[:, VA:VA + 512]).astype(BF16)
    lg = _log_sigmoid(gate_pre + ba_ref[...]) * (1.0 / GATE_TEMP)
    lg_hi, lg_lo = _split_bf16(lg)
    bcum = _mm(tri, lg_hi) + _mm(tri, lg_lo)
    qr = _mm(xn, wr_ref[:, QR:QR + 512])
    kr = _mm(xn, wr_ref[:, KR:KR + 512])
    b_end = bcum[tt - 1:tt, :]

    o_inter = _mm_nt((q * jnp.exp(bcum)).astype(BF16), sgt.astype(BF16))
    k_end = (k * jnp.exp(b_end - bcum)).astype(BF16)
    sgt = jnp.exp(b_end) * sgt + bd_mask * _mm_tn(v, k_end)
    vr = _mm(xn, wr_ref[:, VR:VR + 512]).astype(BF16)
    gate_a = _mm(xn, wa_ref[:, GA:GA + 512])
    gate_r = _mm(xn, wr_ref[:, GR:GR + 512])

    n_chunks = tt // GLA_CHUNK
    chunk = lambda j: slice(j * GLA_CHUNK, (j + 1) * GLA_CHUNK)
    vstacks = [jnp.where(v_head_mask, jnp.concatenate([v[chunk(j)]] * N_HEADS, axis=0), 0.0)
               for j in range(n_chunks)]
    def gla_chunk(ci):
        ref = bcum[ci * GLA_CHUNK - 1:ci * GLA_CHUNK, :] if ci > 0 else jnp.zeros((1, 256), F32)
        qc = (q[chunk(ci)] * jnp.exp(bcum[chunk(ci)] - ref)).astype(BF16)
        o_c = o_inter[chunk(ci)]
        for cj in range(ci + 1):
            kc = k[chunk(cj)] * jnp.exp(ref - bcum[chunk(cj)])
            kstack = (jnp.concatenate([kc] * N_HEADS, axis=0) * k_head_mask).astype(BF16)
            a = _mm_nt(qc, kstack)
            if cj == ci:
                a = jnp.where(causal, a, 0.0)
            o_c = o_c + _mm(a.astype(BF16), vstacks[cj])
        return o_c

    def ret_head(h):
        hs = slice(h * DV, (h + 1) * DV)
        qh = (_rotary(qr[:, hs], cos2, sin2) * gq_ref[h]).astype(BF16)
        kh = (_rotary(kr[:, hs], cos2, sin2) * gk_ref[h]).astype(BF16)
        vh = vr[:, hs]
        a = jnp.where(causal_r, _mm_nt(qh, kh), 0.0).astype(BF16)
        o_h = _mm(qh, sr[h].astype(BF16)) + _mm(a, vh)
        return o_h, math.exp(tt * LOG_GAMMA[h]) * (sr[h] + _mm_tn(kh, vh))

    o_a_chunks, o_r_heads, sr_new = [], [], []
    for i in range(max(n_chunks, N_HEADS)):
        if i < N_HEADS:
            o_h, s_h = ret_head(i)
            o_r_heads.append(o_h)
            sr_new.append(s_h)
        if i < n_chunks:
            o_a_chunks.append(gla_chunk(i))
    o_a = jnp.concatenate(o_a_chunks, axis=0)
    o_r = jnp.concatenate(o_r_heads, axis=1)

    y_a = _head_norm_gate(o_a, gn_ref[:, 0:512], gate_a)
    y_r = _head_norm_gate(o_r, gn_ref[:, 512:1024], gate_r)
    y = jnp.concatenate([y_a, y_r], axis=1).astype(BF16)
    return x + _rms(_mm(y, wout_ref[...]), ng_ref[3:4, :]), sgt, sr_new


def _mixer_prompt_kernel(x_ref, ng_ref, wa_ref, wr_ref, wl_ref, wa2_ref, ba_ref, gn_ref, wout_ref,
                         cos_ref, sin_ref, gq_ref, gk_ref, o_ref, sg_ref, sr_ref, sgt_s, sr_s):
    t_idx = pl.program_id(1)

    @pl.when(t_idx == 0)
    def _():
        sgt_s[...] = jnp.zeros(sgt_s.shape, F32)
        sr_s[...] = jnp.zeros(sr_s.shape, F32)

    sub = MIX_SUB_TILE
    rows4 = N_HEADS * GLA_CHUNK
    r = _iota((sub, sub), 0)
    c = _iota((sub, sub), 1)
    consts = (
        jnp.where(c <= r, 1.0, 0.0).astype(BF16),
        jnp.where(jnp.right_shift(_iota((rows4, 256), 0), 6)
                  == jnp.right_shift(_iota((rows4, 256), 1), 6), 1.0, 0.0),
        jnp.right_shift(_iota((rows4, 512), 0), 6) == jnp.right_shift(_iota((rows4, 512), 1), 7),
        jnp.bitwise_and(_iota((GLA_CHUNK, rows4), 1), GLA_CHUNK - 1) <= _iota((GLA_CHUNK, rows4), 0),
        jnp.where(jnp.right_shift(_iota((512, 256), 0), 7)
                  == jnp.right_shift(_iota((512, 256), 1), 6), 1.0, 0.0),
        c <= r,
    )
    sgt = sgt_s[...]
    sr = [sr_s[h] for h in range(N_HEADS)]
    for s in range(x_ref.shape[0] // sub):
        rs = slice(s * sub, (s + 1) * sub)
        y, sgt, sr = _mixer_rows(x_ref[rs, :], cos_ref[rs, :], sin_ref[rs, :], sgt, sr, consts,
                                 ng_ref, wa_ref, wr_ref, wl_ref, wa2_ref, ba_ref, gn_ref, wout_ref,
                                 gq_ref, gk_ref)
        o_ref[rs, :] = y
    sgt_s[...] = sgt
    for h in range(N_HEADS):
        sr_s[h] = sr[h]

    @pl.when(t_idx == pl.num_programs(1) - 1)
    def _():
        sg = sgt_s[...].T
        for h in range(N_HEADS):
            sg_ref[0, h] = sg[h * DK_GLA:(h + 1) * DK_GLA, h * DV:(h + 1) * DV]
        sr_ref[0] = sr_s[...]


def _mixer_prompt(x_all, batch, seq, tt, layer, norm_g, wa, wr, wl, wa2, ba, gn, wout, cos2, sin2, gq, gk):
    n_t = seq // tt
    x_map = lambda b, t: (b * n_t + t, 0)
    const3 = lambda b, t: (0, 0, 0)
    out_shapes = (
        jax.ShapeDtypeStruct(x_all.shape, F32),
        jax.ShapeDtypeStruct((batch, N_HEADS, DK_GLA, DV), F32),
        jax.ShapeDtypeStruct((batch, N_HEADS, DV, DV), F32),
    )
    weights = (norm_g, wa, wr, wl, wa2, ba, gn, wout)
    return pl.pallas_call(
        _mixer_prompt_kernel,
        out_shape=out_shapes,
        grid=(batch, n_t),
        in_specs=[pl.BlockSpec((tt, D_MODEL), x_map)]
        + [_layer_spec(w, layer, 2) for w in weights]
        + [pl.BlockSpec((tt, DV), lambda b, t: (t, 0)),
           pl.BlockSpec((tt, DV), lambda b, t: (t, 0)),
           pl.BlockSpec(gq.shape, const3),
           pl.BlockSpec(gk.shape, const3)],
        out_specs=(
            pl.BlockSpec((tt, D_MODEL), x_map),
            pl.BlockSpec((1, N_HEADS, DK_GLA, DV), lambda b, t: (b, 0, 0, 0)),
            pl.BlockSpec((1, N_HEADS, DV, DV), lambda b, t: (b, 0, 0, 0)),
        ),
        scratch_shapes=[
            pltpu.VMEM((N_HEADS * DV, N_HEADS * DK_GLA), F32),
            pltpu.VMEM((N_HEADS, DV, DV), F32),
        ],
        input_output_aliases={0: 0},
        compiler_params=_params(2),
        name="mixer_prompt",
    )(x_all, *weights, cos2, sin2, gq, gk)


def _proj_sample_kernel(x_ref, ng_ref, wa_ref, wr_ref, wl_ref, wa2_ref, ba_ref, pa_ref, pr_ref, lg_ref):
    xn = _rms(x_ref[...], ng_ref[2:3, :]).astype(BF16)
    pa_ref[...] = _mm(xn, wa_ref[...])
    pr_ref[...] = _mm(xn, wr_ref[...])
    alo = _mm(xn, wl_ref[...]).astype(BF16)
    lg_ref[...] = _log_sigmoid(_mm(alo, wa2_ref[...]) + ba_ref[...]) * (1.0 / GATE_TEMP)


def _proj_sample(x_all, row_block, n_rows, layer, norm_g, wa, wr, wl, wa2, ba):
    const = lambda i: (0, 0)
    weights = (norm_g, wa, wr, wl, wa2, ba)
    return pl.pallas_call(
        _proj_sample_kernel,
        out_shape=(jax.ShapeDtypeStruct((n_rows, A_COLS), F32),
                   jax.ShapeDtypeStruct((n_rows, R_COLS), F32),
                   jax.ShapeDtypeStruct((n_rows, 256), F32)),
        grid=(1,),
        in_specs=[pl.BlockSpec((n_rows, D_MODEL), lambda i: (row_block, 0))]
        + [_layer_spec(w, layer, 1) for w in weights],
        out_specs=(pl.BlockSpec((n_rows, A_COLS), const),
                   pl.BlockSpec((n_rows, R_COLS), const),
                   pl.BlockSpec((n_rows, 256), const)),
        compiler_params=_params(1),
        name="proj_sample",
    )(x_all, *weights)


def _core_sample_kernel(pa_ref, pr_ref, lg_ref, cos_ref, sin_ref, gq_ref, gk_ref, sg_in, sr_in,
                        o_ref, sg_out, sr_out, *, n_tok):
    n_seq = sg_in.shape[0]
    rows4 = N_HEADS * n_tok
    tok_shift = n_tok.bit_length() - 1
    k_head_mask = jnp.where(jnp.right_shift(_iota((rows4, 256), 0), tok_shift)
                            == jnp.right_shift(_iota((rows4, 256), 1), 6), 1.0, 0.0)
    v_head_mask = (jnp.right_shift(_iota((rows4, 512), 0), tok_shift)
                   == jnp.right_shift(_iota((rows4, 512), 1), 7))
    causal = jnp.bitwise_and(_iota((n_tok, rows4), 1), n_tok - 1) <= _iota((n_tok, rows4), 0)
    causal_r = _iota((n_tok, n_tok), 1) <= _iota((n_tok, n_tok), 0)
    ones = jnp.ones((2 * n_tok, DV), BF16)
    cos2 = cos_ref[...]
    sin2 = sin_ref[...]

    def one_seq(n, pa, pr, lg):
        b_rows = [lg[0:1]]
        for t in range(1, n_tok):
            b_rows.append(b_rows[-1] + lg[t:t + 1])
        bc = jnp.concatenate(b_rows, axis=0)
        b_last = b_rows[-1]
        k = pa[:, KA:KA + 256]
        v = pa[:, VA:VA + 512].astype(BF16)
        qt = (pa[:, QA:QA + 256] * (DK_GLA ** -0.5) * jnp.exp(bc)).astype(BF16)
        kt = k * jnp.exp(-bc)
        kp = (k * jnp.exp(b_last - bc)).astype(BF16)
        s_old = sg_in[n].reshape(N_HEADS * DK_GLA, DV)
        s_old_b = s_old.astype(BF16)
        s_bd = jnp.concatenate(
            [jnp.where(jnp.right_shift(_iota((N_HEADS * DK_GLA, DV), 0), 6) == h, s_old_b, 0.0)
             for h in range(N_HEADS)], axis=1)
        kstack = (jnp.concatenate([kt] * N_HEADS, axis=0) * k_head_mask).astype(BF16)
        vstack = jnp.where(v_head_mask, jnp.concatenate([v] * N_HEADS, axis=0), 0.0)
        a = jnp.where(causal, _mm_nt(qt, kstack), 0.0).astype(BF16)
        o_a = _mm(qt, s_bd) + _mm(a, vstack)
        lg_hi, lg_lo = _split_bf16(lg)
        b_last_t = _mm_tn(jnp.concatenate([lg_hi, lg_lo], axis=0), ones)
        u = _mm_tn(kp, v)
        u_d = jnp.concatenate(
            [u[h * DK_GLA:(h + 1) * DK_GLA, h * DV:(h + 1) * DV] for h in range(N_HEADS)], axis=0)
        sg_out[n] = (jnp.exp(b_last_t) * s_old + u_d).reshape(N_HEADS, DK_GLA, DV)
        vr = pr[:, VR:VR + 512].astype(BF16)
        o_r_heads = []
        for h in range(N_HEADS):
            hs = slice(h * DV, (h + 1) * DV)
            qh = (_rotary(pr[:, QR + h * DV:QR + (h + 1) * DV], cos2, sin2) * gq_ref[h]).astype(BF16)
            kh = (_rotary(pr[:, KR + h * DV:KR + (h + 1) * DV], cos2, sin2) * gk_ref[h]).astype(BF16)
            vh = vr[:, hs]
            ar = jnp.where(causal_r, _mm_nt(qh, kh), 0.0).astype(BF16)
            s_r = sr_in[n, h]
            o_r_heads.append(_mm(qh, s_r.astype(BF16)) + _mm(ar, vh))
            sr_out[n, h] = math.exp(n_tok * LOG_GAMMA[h]) * (s_r + _mm_tn(kh, vh))
        return jnp.concatenate([o_a] + o_r_heads, axis=1)

    seq_per_iter = 8 // n_tok

    def body(i, carry):
        rows = pl.ds(pl.multiple_of(i * 8, 8), 8)
        pa8, pr8, lg8 = pa_ref[rows, :], pr_ref[rows, :], lg_ref[rows, :]
        outs = []
        for j in range(seq_per_iter):
            sl = slice(j * n_tok, (j + 1) * n_tok)
            outs.append(one_seq(i * seq_per_iter + j, pa8[sl], pr8[sl], lg8[sl]))
        o_ref[rows, :] = jnp.concatenate(outs, axis=0)
        return carry

    lax.fori_loop(0, n_seq // seq_per_iter, body, 0)


def _core_sample(pa, pr, lg, layer, n_tok, cos2, sin2, gq, gk, state_gla, state_ret):
    n_seq = state_gla.shape[1]
    nb = SAMPLE_SEQ_BLOCK
    rows = nb * n_tok
    const2 = lambda i: (0, 0)
    const3 = lambda i: (0, 0, 0)
    out_shapes = (
        jax.ShapeDtypeStruct((n_seq * n_tok, D_MODEL), F32),
        jax.ShapeDtypeStruct(state_gla.shape[1:], F32),
        jax.ShapeDtypeStruct(state_ret.shape[1:], F32),
    )
    return pl.pallas_call(
        functools.partial(_core_sample_kernel, n_tok=n_tok),
        out_shape=out_shapes,
        grid=(n_seq // nb,),
        in_specs=[
            pl.BlockSpec((rows, A_COLS), lambda i: (i, 0)),
            pl.BlockSpec((rows, R_COLS), lambda i: (i, 0)),
            pl.BlockSpec((rows, 256), lambda i: (i, 0)),
            pl.BlockSpec(cos2.shape, const2),
            pl.BlockSpec(sin2.shape, const2),
            pl.BlockSpec(gq.shape, const3),
            pl.BlockSpec(gk.shape, const3),
            pl.BlockSpec((None, nb, N_HEADS, DK_GLA, DV), lambda i: (layer, i, 0, 0, 0)),
            pl.BlockSpec((None, nb, N_HEADS, DV, DV), lambda i: (layer, i, 0, 0, 0)),
        ],
        out_specs=(
            pl.BlockSpec((rows, D_MODEL), lambda i: (i, 0)),
            pl.BlockSpec((nb, N_HEADS, DK_GLA, DV), lambda i: (i, 0, 0, 0)),
            pl.BlockSpec((nb, N_HEADS, DV, DV), lambda i: (i, 0, 0, 0)),
        ),
        compiler_params=_params(1),
        name="core_sample",
    )(pa, pr, lg, cos2, sin2, gq, gk, state_gla, state_ret)


def _out_sample_kernel(o_ref, ga_ref, gr_ref, gn_ref, wout_ref, ng_ref, x_ref, y_ref):
    o = o_ref[...]
    y_a = _head_norm_gate(o[:, 0:512], gn_ref[:, 0:512], ga_ref[...])
    y_r = _head_norm_gate(o[:, 512:1024], gn_ref[:, 512:1024], gr_ref[...])
    y = jnp.concatenate([y_a, y_r], axis=1).astype(BF16)
    y_ref[...] = x_ref[...] + _rms(_mm(y, wout_ref[...]), ng_ref[3:4, :])


def _out_sample(o2, pa, pr, layer, gn, wout, norm_g, x_all, row_block):
    n_rows = o2.shape[0]
    const = lambda i: (0, 0)
    return pl.pallas_call(
        _out_sample_kernel,
        out_shape=jax.ShapeDtypeStruct(x_all.shape, F32),
        grid=(1,),
        in_specs=[
            pl.BlockSpec(o2.shape, const),
            pl.BlockSpec((n_rows, 512), lambda i: (0, GA // 512)),
            pl.BlockSpec((n_rows, 512), lambda i: (0, GR // 512)),
            _layer_spec(gn, layer, 1),
            _layer_spec(wout, layer, 1),
            _layer_spec(norm_g, layer, 1),
            pl.BlockSpec((n_rows, D_MODEL), lambda i: (row_block, 0)),
        ],
        out_specs=pl.BlockSpec((n_rows, D_MODEL), lambda i: (row_block, 0)),
        input_output_aliases={6: 0},
        compiler_params=_params(1),
        name="out_sample",
    )(o2, pa, pr, gn, wout, norm_g, x_all)


def _rope_tables(pos):
    half = DV // 2
    inv_freq = ROPE_BASE ** (-jnp.arange(half, dtype=F32) / half)
    ang = pos[:, None] * inv_freq[None, :]
    cos, sin = jnp.cos(ang), jnp.sin(ang)
    return jnp.concatenate([cos, cos], axis=-1), jnp.concatenate([-sin, sin], axis=-1)


def _decay_tables(chunk):
    steps = (jnp.arange(chunk, dtype=F32) + 1.0)[None, :, None]
    lgam = jnp.asarray(LOG_GAMMA, F32)[:, None, None]
    gq = jnp.broadcast_to(jnp.exp(steps * lgam), (N_HEADS, chunk, DV))
    gk = jnp.broadcast_to(jnp.exp(-steps * lgam) * (DV ** -0.5), (N_HEADS, chunk, DV))
    return gq, gk


def _largest_tile(n_rows, sub, max_subs):
    return max(m * sub for m in range(1, max_subs + 1) if n_rows % (m * sub) == 0)


def kernel(x_prompt, x_sample, state_gla, state_ret, norm_g, w1_gu, w1_down, w2_gu, w2_down,
           w_in, w_a2, b_a, gn_gla, gn_ret, w_out):
    batch, seq, _ = x_prompt.shape
    n_seq, n_tok, _ = x_sample.shape
    n_prompt = batch * seq
    n_sample = n_seq * n_tok
    n_all = n_prompt + n_sample
    assert n_prompt % FFN_SUB_TILE == 0 and n_sample % FFN_SUB_TILE == 0
    assert seq % MIX_SUB_TILE == 0 and n_prompt % n_sample == 0
    assert n_seq % SAMPLE_SEQ_BLOCK == 0 and n_tok in (1, 2, 4, 8)
    ffn_tile = _largest_tile(n_all, FFN_SUB_TILE, FFN_SUB_TILES_PER_STEP)
    mix_tile = _largest_tile(seq, MIX_SUB_TILE, MIX_SUB_TILES_PER_STEP)
    sample_block = n_prompt // n_sample

    w1_gu_b, w1_dn_b = w1_gu.astype(BF16), w1_down.astype(BF16)
    w2_gu_b, w2_dn_b = w2_gu.astype(BF16), w2_down.astype(BF16)
    wa_b = w_in[..., :A_COLS].astype(BF16)
    wr_b = w_in[..., A_COLS + GATE_RANK:].astype(BF16)
    wl_b = jnp.pad(w_in[..., A_COLS:A_COLS + GATE_RANK].astype(BF16),
                   ((0, 0), (0, 0), (0, AL_PAD - GATE_RANK)))
    wa2_b = jnp.pad(w_a2.astype(BF16), ((0, 0), (0, AL_PAD - GATE_RANK), (0, 0)))
    w_out_b = w_out.astype(BF16)
    b_a3 = b_a[:, None, :]
    gn = jnp.concatenate([gn_gla, gn_ret], axis=-1)[:, None, :]

    cos_p, sin_p = _rope_tables(jnp.arange(seq, dtype=F32))
    cos_s, sin_s = _rope_tables(PAST_LEN + jnp.arange(n_tok, dtype=F32))
    gq_p, gk_p = _decay_tables(MIX_SUB_TILE)
    gq_s, gk_s = _decay_tables(n_tok)

    xp = x_prompt.reshape(n_prompt, D_MODEL)
    xs = x_sample.reshape(n_sample, D_MODEL)
    gla_p, ret_p, gla_s, ret_s = [], [], [], []
    for l in range(DEPTH):
        if l == 0:
            x_all = _ffn_split_in(xp, xs, l, norm_g, w1_gu_b, w1_dn_b, 0)
        else:
            x_all = _ffn(x_all, l, norm_g, w1_gu_b, w1_dn_b, 0, ffn_tile)
        pa, pr, lg = _proj_sample(x_all, sample_block, n_sample, l, norm_g, wa_b, wr_b, wl_b, wa2_b, b_a3)
        o2, sg_s, sr_s = _core_sample(pa, pr, lg, l, n_tok, cos_s, sin_s, gq_s, gk_s, state_gla, state_ret)
        x_all, sg_p, sr_p = _mixer_prompt(x_all, batch, seq, mix_tile, l, norm_g, wa_b, wr_b, wl_b, wa2_b,
                                          b_a3, gn, w_out_b, cos_p, sin_p, gq_p, gk_p)
        x_all = _out_sample(o2, pa, pr, l, gn, w_out_b, norm_g, x_all, sample_block)
        if l == DEPTH - 1:
            yp, ys = _ffn_split_out(x_all, n_prompt, l, norm_g, w2_gu_b, w2_dn_b, 4)
        else:
            x_all = _ffn(x_all, l, norm_g, w2_gu_b, w2_dn_b, 4, ffn_tile)
        gla_p.append(sg_p)
        ret_p.append(sr_p)
        gla_s.append(sg_s)
        ret_s.append(sr_s)

    return (yp.reshape(batch, seq, D_MODEL), ys.reshape(n_seq, n_tok, D_MODEL),
            jnp.stack(gla_p), jnp.stack(ret_p), jnp.stack(gla_s), jnp.stack(ret_s))
```

```python
import functools
import math

import jax
import jax.numpy as jnp
from jax import lax
from jax.experimental import pallas as pl
from jax.experimental.pallas import tpu as pltpu

F32 = jnp.float32
BF16 = jnp.bfloat16

D_MODEL = 1024
DEPTH = 4
PAST_LEN = 16384
N_HEADS = 4
DK_GLA = 64
DV = 128
GATE_RANK = 16
GATE_TEMP = 16.0
D_FF = 2816
ROPE_BASE = 10000.0
EPS = 1e-6
GLA_CHUNK = 64

A_COLS = 1536
R_COLS = 2048
QA, KA, VA, GA = 0, 256, 512, 1024
QR, KR, VR, GR = 0, 512, 1024, 1536
AL_PAD = 128

FFN_SUB_TILE = 512
FFN_FF_TILE = 256
MIX_SUB_TILE = 256
FFN_SUB_TILES_PER_STEP = 1
MIX_SUB_TILES_PER_STEP = 1
SAMPLE_SEQ_BLOCK = 16
VMEM_LIMIT_BYTES = 56 * 1024 * 1024

LOG_GAMMA = tuple(math.log1p(-(2.0 ** (-5.0 - h))) for h in range(N_HEADS))


def _mm(a, b):
    return jnp.dot(a, b, preferred_element_type=F32)


def _mm_nt(a, b):
    return lax.dot_general(a, b, (((1,), (1,)), ((), ())), preferred_element_type=F32)


def _mm_tn(a, b):
    return lax.dot_general(a, b, (((0,), (0,)), ((), ())), preferred_element_type=F32)


def _rms(x, g):
    ms = jnp.mean(x * x, axis=-1, keepdims=True)
    return x * lax.rsqrt(ms + EPS) * g


def _silu(x):
    return x * (1.0 / (1.0 + jnp.exp(-x)))


def _log_sigmoid(x):
    return jnp.minimum(x, 0.0) - jnp.log1p(jnp.exp(-jnp.abs(x)))


def _split_bf16(x):
    hi = x.astype(BF16)
    lo = (x - hi.astype(F32)).astype(BF16)
    return hi, lo


def _iota(shape, dim):
    return lax.broadcasted_iota(jnp.int32, shape, dim)


def _rotary(x, cos2, sin2):
    return x * cos2 + pltpu.roll(x, DV // 2, axis=1) * sin2


def _head_norm_gate(o, gn, gate):
    outs = []
    for h in range(N_HEADS):
        oh = o[:, h * DV:(h + 1) * DV]
        ms = jnp.mean(oh * oh, axis=-1, keepdims=True)
        outs.append(oh * lax.rsqrt(ms + EPS))
    return jnp.concatenate(outs, axis=1) * gn * _silu(gate)


def _first_of_block(x, block, row):
    bit = 1
    while bit < block:
        x = jnp.where(jnp.bitwise_and(row, bit) != 0, pltpu.roll(x, bit, axis=0), x)
        bit *= 2
    return x


def _stack_heads(x, k_head_mask_b):
    return jnp.concatenate([x.astype(BF16)] * N_HEADS, axis=0) * k_head_mask_b


def _gla_chunk_scores(qv, kv, bv, lgv, k_head_mask_b):
    n = qv.shape[0]
    row = _iota((n, 256), 0)
    pos16, pos4, pos1 = jnp.right_shift(row, 4), jnp.bitwise_and(jnp.right_shift(row, 2), 3), jnp.bitwise_and(row, 3)
    b_first16 = jnp.concatenate([jnp.broadcast_to(bv[16 * g:16 * g + 1, :], (16, 256))
                                 for g in range(n // 16)], axis=0)
    q1 = qv * jnp.exp(bv - b_first16)
    q2 = qv * jnp.exp(bv - _first_of_block(bv, 4, row))

    def level(q_lvl, pos, k_slots, first_slot):
        q_b = q_lvl.astype(BF16)
        lhs = [q_b * jnp.where(pos == j, 1.0, 0.0).astype(BF16) for j in range(first_slot, 4)]
        rhs = [_stack_heads(k_j, k_head_mask_b) for k_j in k_slots]
        return _mm_nt(jnp.concatenate(lhs, axis=1), jnp.concatenate(rhs, axis=1))

    k1 = [jnp.where(row < 16 * j, kv * jnp.exp(bv[16 * j:16 * j + 1, :] - bv), 0.0) for j in range(1, 4)]
    s1 = level(q1, pos16, k1, 1)
    k2 = []
    for j in range(1, 4):
        ref = jnp.concatenate([jnp.broadcast_to(bv[16 * g + 4 * j:16 * g + 4 * j + 1, :], (16, 256))
                               for g in range(n // 16)], axis=0)
        k2.append(jnp.where(jnp.bitwise_and(row, 15) < 4 * j, kv * jnp.exp(ref - bv), 0.0))
    s2 = level(q2, pos4, k2, 1)
    f1 = pltpu.roll(lgv, n - 1, axis=0)
    f2 = f1 + pltpu.roll(lgv, n - 2, axis=0)
    f3 = f2 + pltpu.roll(lgv, n - 3, axis=0)
    ahead = [kv, kv * jnp.exp(f1), kv * jnp.exp(f2), kv * jnp.exp(f3)]
    k3 = []
    for j in range(4):
        k_j = jnp.zeros_like(kv)
        for p in range(j + 1):
            k_j = jnp.where(pos1 == p, ahead[j - p], k_j)
        k3.append(k_j)
    s3 = level(qv, pos1, k3, 0)
    t_idx = _iota((n, N_HEADS * n), 0)
    s_idx = jnp.bitwise_and(_iota((n, N_HEADS * n), 1), n - 1)
    same16 = jnp.right_shift(t_idx, 4) == jnp.right_shift(s_idx, 4)
    same4 = jnp.right_shift(t_idx, 2) == jnp.right_shift(s_idx, 2)
    return s1 + jnp.where(same16, s2, 0.0) + jnp.where(same4, s3, 0.0)


def _layer_spec(arr, layer, n_grid_axes):
    zeros = (0,) * (arr.ndim - 1)
    if n_grid_axes == 1:
        return pl.BlockSpec((None,) + arr.shape[1:], lambda i: (layer,) + zeros)
    return pl.BlockSpec((None,) + arr.shape[1:], lambda i, j: (layer,) + zeros)


def _params(n_axes):
    return pltpu.CompilerParams(dimension_semantics=("arbitrary",) * n_axes,
                                vmem_limit_bytes=VMEM_LIMIT_BYTES)


def _ffn_rows(x, ng_ref, wgu_ref, wdn_ref, pre_row):
    xn = _rms(x, ng_ref[pre_row:pre_row + 1, :]).astype(BF16)
    acc = jnp.zeros(x.shape, F32)
    for c in range(D_FF // FFN_FF_TILE):
        lo = c * FFN_FF_TILE
        gate = _mm(xn, wgu_ref[:, lo:lo + FFN_FF_TILE])
        up = _mm(xn, wgu_ref[:, D_FF + lo:D_FF + lo + FFN_FF_TILE])
        act = (_silu(gate) * up).astype(BF16)
        acc = acc + _mm(act, wdn_ref[lo:lo + FFN_FF_TILE, :])
    return x + 0.5 * _rms(acc, ng_ref[pre_row + 1:pre_row + 2, :])


def _ffn_kernel(x_ref, ng_ref, wgu_ref, wdn_ref, o_ref, *, pre_row):
    for s in range(x_ref.shape[0] // FFN_SUB_TILE):
        rs = slice(s * FFN_SUB_TILE, (s + 1) * FFN_SUB_TILE)
        o_ref[rs, :] = _ffn_rows(x_ref[rs, :], ng_ref, wgu_ref, wdn_ref, pre_row)


def _ffn_split_in_kernel(xp_ref, xs_ref, ng_ref, wgu_ref, wdn_ref, o_ref, *, pre_row, n_prompt_tiles):
    x = jnp.where(pl.program_id(0) < n_prompt_tiles, xp_ref[...], xs_ref[...])
    o_ref[...] = _ffn_rows(x, ng_ref, wgu_ref, wdn_ref, pre_row)


def _ffn_split_out_kernel(x_ref, ng_ref, wgu_ref, wdn_ref, op_ref, os_ref, *, pre_row, n_prompt_tiles):
    y = _ffn_rows(x_ref[...], ng_ref, wgu_ref, wdn_ref, pre_row)
    is_prompt = pl.program_id(0) < n_prompt_tiles

    @pl.when(is_prompt)
    def _():
        op_ref[...] = y

    @pl.when(jnp.logical_not(is_prompt))
    def _():
        os_ref[...] = y


def _ffn(x_all, layer, norm_g, wgu, wdn, pre_row, tile):
    return pl.pallas_call(
        functools.partial(_ffn_kernel, pre_row=pre_row),
        out_shape=jax.ShapeDtypeStruct(x_all.shape, F32),
        grid=(x_all.shape[0] // tile,),
        in_specs=[pl.BlockSpec((tile, D_MODEL), lambda i: (i, 0)),
                  _layer_spec(norm_g, layer, 1), _layer_spec(wgu, layer, 1), _layer_spec(wdn, layer, 1)],
        out_specs=pl.BlockSpec((tile, D_MODEL), lambda i: (i, 0)),
        compiler_params=_params(1),
        name="ffn",
    )(x_all, norm_g, wgu, wdn)


def _ffn_split_in(xp, xs, layer, norm_g, wgu, wdn, pre_row):
    tm = FFN_SUB_TILE
    n_p, n_s = xp.shape[0] // tm, xs.shape[0] // tm
    return pl.pallas_call(
        functools.partial(_ffn_split_in_kernel, pre_row=pre_row, n_prompt_tiles=n_p),
        out_shape=jax.ShapeDtypeStruct((xp.shape[0] + xs.shape[0], D_MODEL), F32),
        grid=(n_p + n_s,),
        in_specs=[pl.BlockSpec((tm, D_MODEL), lambda i: (jnp.minimum(i, n_p - 1), 0)),
                  pl.BlockSpec((tm, D_MODEL), lambda i: (jnp.maximum(i - n_p, 0), 0)),
                  _layer_spec(norm_g, layer, 1), _layer_spec(wgu, layer, 1), _layer_spec(wdn, layer, 1)],
        out_specs=pl.BlockSpec((tm, D_MODEL), lambda i: (i, 0)),
        compiler_params=_params(1),
        name="ffn_split_in",
    )(xp, xs, norm_g, wgu, wdn)


def _ffn_split_out(x_all, n_prompt, layer, norm_g, wgu, wdn, pre_row):
    tm = FFN_SUB_TILE
    n_p, n_s = n_prompt // tm, (x_all.shape[0] - n_prompt) // tm
    return pl.pallas_call(
        functools.partial(_ffn_split_out_kernel, pre_row=pre_row, n_prompt_tiles=n_p),
        out_shape=(jax.ShapeDtypeStruct((n_prompt, D_MODEL), F32),
                   jax.ShapeDtypeStruct((x_all.shape[0] - n_prompt, D_MODEL), F32)),
        grid=(n_p + n_s,),
        in_specs=[pl.BlockSpec((tm, D_MODEL), lambda i: (i, 0)),
                  _layer_spec(norm_g, layer, 1), _layer_spec(wgu, layer, 1), _layer_spec(wdn, layer, 1)],
        out_specs=(pl.BlockSpec((tm, D_MODEL), lambda i: (jnp.minimum(i, n_p - 1), 0)),
                   pl.BlockSpec((tm, D_MODEL), lambda i: (jnp.maximum(i - n_p, 0), 0))),
        compiler_params=_params(1),
        name="ffn_split_out",
    )(x_all, norm_g, wgu, wdn)


def _mixer_rows(x, cos2, sin2, sgt, sr, consts, ng_ref, wa_ref, wr_ref, wl_ref, wa2_ref, ba_ref,
                gn_ref, wout_ref, gq_ref, gk_ref):
    tri, k_head_mask, v_head_mask, bd_mask, causal_r = consts
    tt = x.shape[0]
    xn = _rms(x, ng_ref[2:3, :]).astype(BF16)

    alo = _mm(xn, wl_ref[...]).astype(BF16)
    gate_pre = _mm(alo, wa2_ref[...])
    q = _mm(xn, wa_ref[:, QA:QA + 256]) * (DK_GLA ** -0.5)
    k = _mm(xn, wa_ref[:, KA:KA + 256])
    v = _mm(xn, wa_ref[:, VA:VA + 512]).astype(BF16)
    lg = _log_sigmoid(gate_pre + ba_ref[...]) * (1.0 / GATE_TEMP)
    lg_hi, lg_lo = _split_bf16(lg)
    bcum = _mm(tri, lg_hi) + _mm(tri, lg_lo)
    qr = _mm(xn, wr_ref[:, QR:QR + 512])
    kr = _mm(xn, wr_ref[:, KR:KR + 512])
    b_end = bcum[tt - 1:tt, :]

    o_inter = _mm_nt((q * jnp.exp(bcum)).astype(BF16), sgt.astype(BF16))
    k_end = (k * jnp.exp(b_end - bcum)).astype(BF16)
    sgt = jnp.exp(b_end) * sgt + bd_mask * _mm_tn(v, k_end)
    vr = _mm(xn, wr_ref[:, VR:VR + 512]).astype(BF16)

    n_chunks = tt // GLA_CHUNK
    chunk = lambda j: slice(j * GLA_CHUNK, (j + 1) * GLA_CHUNK)
    vstacks = [jnp.where(v_head_mask, jnp.concatenate([v[chunk(j)]] * N_HEADS, axis=0), 0.0)
               for j in range(n_chunks)]
    def gla_scores(ci):
        ref = bcum[ci * GLA_CHUNK - 1:ci * GLA_CHUNK, :] if ci > 0 else jnp.zeros((1, 256), F32)
        qc = (q[chunk(ci)] * jnp.exp(bcum[chunk(ci)] - ref)).astype(BF16)
        scores = []
        for cj in range(ci):
            kc = k[chunk(cj)] * jnp.exp(ref - bcum[chunk(cj)])
            scores.append(_mm_nt(qc, _stack_heads(kc, k_head_mask)))
        scores.append(_gla_chunk_scores(q[chunk(ci)], k[chunk(ci)], bcum[chunk(ci)], lg[chunk(ci)],
                                        k_head_mask))
        return scores

    def gla_values(ci, scores):
        o_c = o_inter[chunk(ci)]
        for cj, a in enumerate(scores):
            o_c = o_c + _mm(a.astype(BF16), vstacks[cj])
        return o_c

    def ret_scores(h):
        hs = slice(h * DV, (h + 1) * DV)
        qh = (_rotary(qr[:, hs], cos2, sin2) * gq_ref[h]).astype(BF16)
        kh = (_rotary(kr[:, hs], cos2, sin2) * gk_ref[h]).astype(BF16)
        vh = vr[:, hs]
        a = _mm_nt(qh, kh)
        o_state = _mm(qh, sr[h].astype(BF16))
        s_new = math.exp(tt * LOG_GAMMA[h]) * (sr[h] + _mm_tn(kh, vh))
        return a, o_state, s_new, vh

    def ret_values(h, staged):
        a, o_state, _, vh = staged
        return o_state + _mm(jnp.where(causal_r, a, 0.0).astype(BF16), vh)

    items = []
    for i in range(max(n_chunks, N_HEADS)):
        if i < N_HEADS:
            items.append(("ret", i))
        if i < n_chunks:
            items.append(("gla", i))
    gate_cols = [(wa_ref, GA), (wa_ref, GA + 256), (wr_ref, GR), (wr_ref, GR + 256)]
    gate_parts = []
    depth = 1
    staged, o_a_chunks, o_r_heads, sr_new = {}, [None] * n_chunks, [None] * N_HEADS, [None] * N_HEADS
    for step in range(len(items) + depth):
        if step < len(items):
            kind, i = items[step]
            staged[(kind, i)] = ret_scores(i) if kind == "ret" else gla_scores(i)
        if step % 2 == 1 and len(gate_parts) < len(gate_cols):
            w_ref, off = gate_cols[len(gate_parts)]
            gate_parts.append(_mm(xn, w_ref[:, off:off + 256]))
        if step >= depth:
            kind, i = items[step - depth]
            if kind == "ret":
                o_r_heads[i] = ret_values(i, staged[(kind, i)])
                sr_new[i] = staged[(kind, i)][2]
            else:
                o_a_chunks[i] = gla_values(i, staged[(kind, i)])
    while len(gate_parts) < len(gate_cols):
        w_ref, off = gate_cols[len(gate_parts)]
        gate_parts.append(_mm(xn, w_ref[:, off:off + 256]))
    gate_a = jnp.concatenate(gate_parts[0:2], axis=1)
    gate_r = jnp.concatenate(gate_parts[2:4], axis=1)
    o_a = jnp.concatenate(o_a_chunks, axis=0)
    o_r = jnp.concatenate(o_r_heads, axis=1)

    y_a = _head_norm_gate(o_a, gn_ref[:, 0:512], gate_a)
    y_r = _head_norm_gate(o_r, gn_ref[:, 512:1024], gate_r)
    y = jnp.concatenate([y_a, y_r], axis=1).astype(BF16)
    return x + _rms(_mm(y, wout_ref[...]), ng_ref[3:4, :]), sgt, sr_new


def _mixer_prompt_kernel(x_ref, ng_ref, wa_ref, wr_ref, wl_ref, wa2_ref, ba_ref, gn_ref, wout_ref,
                         cos_ref, sin_ref, gq_ref, gk_ref, o_ref, sg_ref, sr_ref, sgt_s, sr_s):
    t_idx = pl.program_id(1)

    @pl.when(t_idx == 0)
    def _():
        sgt_s[...] = jnp.zeros(sgt_s.shape, F32)
        sr_s[...] = jnp.zeros(sr_s.shape, F32)

    sub = MIX_SUB_TILE
    rows4 = N_HEADS * GLA_CHUNK
    r = _iota((sub, sub), 0)
    c = _iota((sub, sub), 1)
    consts = (
        jnp.where(c <= r, 1.0, 0.0).astype(BF16),
        jnp.where(jnp.right_shift(_iota((rows4, 256), 0), 6)
                  == jnp.right_shift(_iota((rows4, 256), 1), 6), 1.0, 0.0).astype(BF16),
        jnp.right_shift(_iota((rows4, 512), 0), 6) == jnp.right_shift(_iota((rows4, 512), 1), 7),
        jnp.where(jnp.right_shift(_iota((512, 256), 0), 7)
                  == jnp.right_shift(_iota((512, 256), 1), 6), 1.0, 0.0),
        c <= r,
    )
    sgt = sgt_s[...]
    sr = [sr_s[h] for h in range(N_HEADS)]
    for s in range(x_ref.shape[0] // sub):
        rs = slice(s * sub, (s + 1) * sub)
        y, sgt, sr = _mixer_rows(x_ref[rs, :], cos_ref[rs, :], sin_ref[rs, :], sgt, sr, consts,
                                 ng_ref, wa_ref, wr_ref, wl_ref, wa2_ref, ba_ref, gn_ref, wout_ref,
                                 gq_ref, gk_ref)
        o_ref[rs, :] = y
    sgt_s[...] = sgt
    for h in range(N_HEADS):
        sr_s[h] = sr[h]

    @pl.when(t_idx == pl.num_programs(1) - 1)
    def _():
        sg = sgt_s[...].T
        for h in range(N_HEADS):
            sg_ref[0, h] = sg[h * DK_GLA:(h + 1) * DK_GLA, h * DV:(h + 1) * DV]
        sr_ref[0] = sr_s[...]


def _mixer_prompt(x_all, batch, seq, tt, layer, norm_g, wa, wr, wl, wa2, ba, gn, wout, cos2, sin2, gq, gk):
    n_t = seq // tt
    x_map = lambda b, t: (b * n_t + t, 0)
    const3 = lambda b, t: (0, 0, 0)
    out_shapes = (
        jax.ShapeDtypeStruct(x_all.shape, F32),
        jax.ShapeDtypeStruct((batch, N_HEADS, DK_GLA, DV), F32),
        jax.ShapeDtypeStruct((batch, N_HEADS, DV, DV), F32),
    )
    weights = (norm_g, wa, wr, wl, wa2, ba, gn, wout)
    return pl.pallas_call(
        _mixer_prompt_kernel,
        out_shape=out_shapes,
        grid=(batch, n_t),
        in_specs=[pl.BlockSpec((tt, D_MODEL), x_map)]
        + [_layer_spec(w, layer, 2) for w in weights]
        + [pl.BlockSpec((tt, DV), lambda b, t: (t, 0)),
           pl.BlockSpec((tt, DV), lambda b, t: (t, 0)),
           pl.BlockSpec(gq.shape, const3),
           pl.BlockSpec(gk.shape, const3)],
        out_specs=(
            pl.BlockSpec((tt, D_MODEL), x_map),
            pl.BlockSpec((1, N_HEADS, DK_GLA, DV), lambda b, t: (b, 0, 0, 0)),
            pl.BlockSpec((1, N_HEADS, DV, DV), lambda b, t: (b, 0, 0, 0)),
        ),
        scratch_shapes=[
            pltpu.VMEM((N_HEADS * DV, N_HEADS * DK_GLA), F32),
            pltpu.VMEM((N_HEADS, DV, DV), F32),
        ],
        input_output_aliases={0: 0},
        compiler_params=_params(2),
        name="mixer_prompt",
    )(x_all, *weights, cos2, sin2, gq, gk)


def _proj_sample_kernel(x_ref, ng_ref, wa_ref, wr_ref, wl_ref, wa2_ref, ba_ref, pa_ref, pr_ref, lg_ref):
    xn = _rms(x_ref[...], ng_ref[2:3, :]).astype(BF16)
    pa_ref[...] = _mm(xn, wa_ref[...])
    pr_ref[...] = _mm(xn, wr_ref[...])
    alo = _mm(xn, wl_ref[...]).astype(BF16)
    lg_ref[...] = _log_sigmoid(_mm(alo, wa2_ref[...]) + ba_ref[...]) * (1.0 / GATE_TEMP)


def _proj_sample(x_all, row_block, n_rows, layer, norm_g, wa, wr, wl, wa2, ba):
    const = lambda i: (0, 0)
    weights = (norm_g, wa, wr, wl, wa2, ba)
    return pl.pallas_call(
        _proj_sample_kernel,
        out_shape=(jax.ShapeDtypeStruct((n_rows, A_COLS), F32),
                   jax.ShapeDtypeStruct((n_rows, R_COLS), F32),
                   jax.ShapeDtypeStruct((n_rows, 256), F32)),
        grid=(1,),
        in_specs=[pl.BlockSpec((n_rows, D_MODEL), lambda i: (row_block, 0))]
        + [_layer_spec(w, layer, 1) for w in weights],
        out_specs=(pl.BlockSpec((n_rows, A_COLS), const),
                   pl.BlockSpec((n_rows, R_COLS), const),
                   pl.BlockSpec((n_rows, 256), const)),
        compiler_params=_params(1),
        name="proj_sample",
    )(x_all, *weights)


def _core_sample_kernel(pa_ref, pr_ref, lg_ref, cos_ref, sin_ref, gq_ref, gk_ref, sg_in, sr_in,
                        o_ref, sg_out, sr_out, *, n_tok):
    n_seq = sg_in.shape[0]
    rows4 = N_HEADS * n_tok
    tok_shift = n_tok.bit_length() - 1
    k_head_mask = jnp.where(jnp.right_shift(_iota((rows4, 256), 0), tok_shift)
                            == jnp.right_shift(_iota((rows4, 256), 1), 6), 1.0, 0.0).astype(BF16)
    v_head_mask = (jnp.right_shift(_iota((rows4, 512), 0), tok_shift)
                   == jnp.right_shift(_iota((rows4, 512), 1), 7))
    tok_row = _iota((n_tok, 256), 0)
    causal_r = _iota((n_tok, n_tok), 1) <= _iota((n_tok, n_tok), 0)
    ones = jnp.ones((2 * n_tok, DV), BF16)
    cos2 = cos_ref[...]
    sin2 = sin_ref[...]

    def one_seq(n, pa, pr, lg):
        b_rows = [lg[0:1]]
        for t in range(1, n_tok):
            b_rows.append(b_rows[-1] + lg[t:t + 1])
        bc = jnp.concatenate(b_rows, axis=0)
        b_last = b_rows[-1]
        k = pa[:, KA:KA + 256]
        v = pa[:, VA:VA + 512].astype(BF16)
        q = pa[:, QA:QA + 256] * (DK_GLA ** -0.5)
        qt = (q * jnp.exp(bc)).astype(BF16)
        kp = (k * jnp.exp(b_last - bc)).astype(BF16)
        s_old = sg_in[n].reshape(N_HEADS * DK_GLA, DV)
        s_old_b = s_old.astype(BF16)
        s_bd = jnp.concatenate(
            [jnp.where(jnp.right_shift(_iota((N_HEADS * DK_GLA, DV), 0), 6) == h, s_old_b, 0.0)
             for h in range(N_HEADS)], axis=1)
        vstack = jnp.where(v_head_mask, jnp.concatenate([v] * N_HEADS, axis=0), 0.0)
        lhs = [jnp.where(tok_row == j, q, 0.0).astype(BF16) for j in range(n_tok)]
        rhs = [_stack_heads(jnp.where(tok_row <= j, k * jnp.exp(b_rows[j] - bc), 0.0), k_head_mask)
               for j in range(n_tok)]
        a = _mm_nt(jnp.concatenate(lhs, axis=1), jnp.concatenate(rhs, axis=1)).astype(BF16)
        o_a = _mm(qt, s_bd) + _mm(a, vstack)
        lg_hi, lg_lo = _split_bf16(lg)
        b_last_t = _mm_tn(jnp.concatenate([lg_hi, lg_lo], axis=0), ones)
        u = _mm_tn(kp, v)
        u_d = jnp.concatenate(
            [u[h * DK_GLA:(h + 1) * DK_GLA, h * DV:(h + 1) * DV] for h in range(N_HEADS)], axis=0)
        sg_out[n] = (jnp.exp(b_last_t) * s_old + u_d).reshape(N_HEADS, DK_GLA, DV)
        vr = pr[:, VR:VR + 512].astype(BF16)
        o_r_heads = []
        for h in range(N_HEADS):
            hs = slice(h * DV, (h + 1) * DV)
            qh = (_rotary(pr[:, QR + h * DV:QR + (h + 1) * DV], cos2, sin2) * gq_ref[h]).astype(BF16)
            kh = (_rotary(pr[:, KR + h * DV:KR + (h + 1) * DV], cos2, sin2) * gk_ref[h]).astype(BF16)
            vh = vr[:, hs]
            ar = jnp.where(causal_r, _mm_nt(qh, kh), 0.0).astype(BF16)
            s_r = sr_in[n, h]
            o_r_heads.append(_mm(qh, s_r.astype(BF16)) + _mm(ar, vh))
            sr_out[n, h] = math.exp(n_tok * LOG_GAMMA[h]) * (s_r + _mm_tn(kh, vh))
        return jnp.concatenate([o_a] + o_r_heads, axis=1)

    seq_per_iter = 8 // n_tok

    def body(i, carry):
        rows = pl.ds(pl.multiple_of(i * 8, 8), 8)
        pa8, pr8, lg8 = pa_ref[rows, :], pr_ref[rows, :], lg_ref[rows, :]
        outs = []
        for j in range(seq_per_iter):
            sl = slice(j * n_tok, (j + 1) * n_tok)
            outs.append(one_seq(i * seq_per_iter + j, pa8[sl], pr8[sl], lg8[sl]))
        o_ref[rows, :] = jnp.concatenate(outs, axis=0)
        return carry

    lax.fori_loop(0, n_seq // seq_per_iter, body, 0)


def _core_sample(pa, pr, lg, layer, n_tok, cos2, sin2, gq, gk, state_gla, state_ret):
    n_seq = state_gla.shape[1]
    nb = SAMPLE_SEQ_BLOCK
    rows = nb * n_tok
    const2 = lambda i: (0, 0)
    const3 = lambda i: (0, 0, 0)
    out_shapes = (
        jax.ShapeDtypeStruct((n_seq * n_tok, D_MODEL), F32),
        jax.ShapeDtypeStruct(state_gla.shape[1:], F32),
        jax.ShapeDtypeStruct(state_ret.shape[1:], F32),
    )
    return pl.pallas_call(
        functools.partial(_core_sample_kernel, n_tok=n_tok),
        out_shape=out_shapes,
        grid=(n_seq // nb,),
        in_specs=[
            pl.BlockSpec((rows, A_COLS), lambda i: (i, 0)),
            pl.BlockSpec((rows, R_COLS), lambda i: (i, 0)),
            pl.BlockSpec((rows, 256), lambda i: (i, 0)),
            pl.BlockSpec(cos2.shape, const2),
            pl.BlockSpec(sin2.shape, const2),
            pl.BlockSpec(gq.shape, const3),
            pl.BlockSpec(gk.shape, const3),
            pl.BlockSpec((None, nb, N_HEADS, DK_GLA, DV), lambda i: (layer, i, 0, 0, 0)),
            pl.BlockSpec((None, nb, N_HEADS, DV, DV), lambda i: (layer, i, 0, 0, 0)),
        ],
        out_specs=(
            pl.BlockSpec((rows, D_MODEL), lambda i: (i, 0)),
            pl.BlockSpec((nb, N_HEADS, DK_GLA, DV), lambda i: (i, 0, 0, 0)),
            pl.BlockSpec((nb, N_HEADS, DV, DV), lambda i: (i, 0, 0, 0)),
        ),
        compiler_params=_params(1),
        name="core_sample",
    )(pa, pr, lg, cos2, sin2, gq, gk, state_gla, state_ret)


def _out_sample_kernel(o_ref, ga_ref, gr_ref, gn_ref, wout_ref, ng_ref, x_ref, y_ref):
    o = o_ref[...]
    y_a = _head_norm_gate(o[:, 0:512], gn_ref[:, 0:512], ga_ref[...])
    y_r = _head_norm_gate(o[:, 512:1024], gn_ref[:, 512:1024], gr_ref[...])
    y = jnp.concatenate([y_a, y_r], axis=1).astype(BF16)
    y_ref[...] = x_ref[...] + _rms(_mm(y, wout_ref[...]), ng_ref[3:4, :])


def _out_sample(o2, pa, pr, layer, gn, wout, norm_g, x_all, row_block):
    n_rows = o2.shape[0]
    const = lambda i: (0, 0)
    return pl.pallas_call(
        _out_sample_kernel,
        out_shape=jax.ShapeDtypeStruct(x_all.shape, F32),
        grid=(1,),
        in_specs=[
            pl.BlockSpec(o2.shape, const),
            pl.BlockSpec((n_rows, 512), lambda i: (0, GA // 512)),
            pl.BlockSpec((n_rows, 512), lambda i: (0, GR // 512)),
            _layer_spec(gn, layer, 1),
            _layer_spec(wout, layer, 1),
            _layer_spec(norm_g, layer, 1),
            pl.BlockSpec((n_rows, D_MODEL), lambda i: (row_block, 0)),
        ],
        out_specs=pl.BlockSpec((n_rows, D_MODEL), lambda i: (row_block, 0)),
        input_output_aliases={6: 0},
        compiler_params=_params(1),
        name="out_sample",
    )(o2, pa, pr, gn, wout, norm_g, x_all)


def _rope_tables(pos):
    half = DV // 2
    inv_freq = ROPE_BASE ** (-jnp.arange(half, dtype=F32) / half)
    ang = pos[:, None] * inv_freq[None, :]
    cos, sin = jnp.cos(ang), jnp.sin(ang)
    return jnp.concatenate([cos, cos], axis=-1), jnp.concatenate([-sin, sin], axis=-1)


def _decay_tables(chunk):
    steps = (jnp.arange(chunk, dtype=F32) + 1.0)[None, :, None]
    lgam = jnp.asarray(LOG_GAMMA, F32)[:, None, None]
    gq = jnp.broadcast_to(jnp.exp(steps * lgam), (N_HEADS, chunk, DV))
    gk = jnp.broadcast_to(jnp.exp(-steps * lgam) * (DV ** -0.5), (N_HEADS, chunk, DV))
    return gq, gk


def _largest_tile(n_rows, sub, max_subs):
    return max(m * sub for m in range(1, max_subs + 1) if n_rows % (m * sub) == 0)


def kernel(x_prompt, x_sample, state_gla, state_ret, norm_g, w1_gu, w1_down, w2_gu, w2_down,
           w_in, w_a2, b_a, gn_gla, gn_ret, w_out):
    batch, seq, _ = x_prompt.shape
    n_seq, n_tok, _ = x_sample.shape
    n_prompt = batch * seq
    n_sample = n_seq * n_tok
    n_all = n_prompt + n_sample
    assert n_prompt % FFN_SUB_TILE == 0 and n_sample % FFN_SUB_TILE == 0
    assert seq % MIX_SUB_TILE == 0 and n_prompt % n_sample == 0
    assert n_seq % SAMPLE_SEQ_BLOCK == 0 and n_tok in (1, 2, 4, 8)
    ffn_tile = _largest_tile(n_all, FFN_SUB_TILE, FFN_SUB_TILES_PER_STEP)
    mix_tile = _largest_tile(seq, MIX_SUB_TILE, MIX_SUB_TILES_PER_STEP)
    sample_block = n_prompt // n_sample

    w1_gu_b, w1_dn_b = w1_gu.astype(BF16), w1_down.astype(BF16)
    w2_gu_b, w2_dn_b = w2_gu.astype(BF16), w2_down.astype(BF16)
    wa_b = w_in[..., :A_COLS].astype(BF16)
    wr_b = w_in[..., A_COLS + GATE_RANK:].astype(BF16)
    wl_b = jnp.pad(w_in[..., A_COLS:A_COLS + GATE_RANK].astype(BF16),
                   ((0, 0), (0, 0), (0, AL_PAD - GATE_RANK)))
    wa2_b = jnp.pad(w_a2.astype(BF16), ((0, 0), (0, AL_PAD - GATE_RANK), (0, 0)))
    w_out_b = w_out.astype(BF16)
    b_a3 = b_a[:, None, :]
    gn = jnp.concatenate([gn_gla, gn_ret], axis=-1)[:, None, :]

    cos_p, sin_p = _rope_tables(jnp.arange(seq, dtype=F32))
    cos_s, sin_s = _rope_tables(PAST_LEN + jnp.arange(n_tok, dtype=F32))
    gq_p, gk_p = _decay_tables(MIX_SUB_TILE)
    gq_s, gk_s = _decay_tables(n_tok)

    xp = x_prompt.reshape(n_prompt, D_MODEL)
    xs = x_sample.reshape(n_sample, D_MODEL)
    gla_p, ret_p, gla_s, ret_s = [], [], [], []
    for l in range(DEPTH):
        if l == 0:
            x_all = _ffn_split_in(xp, xs, l, norm_g, w1_gu_b, w1_dn_b, 0)
        else:
            x_all = _ffn(x_all, l, norm_g, w1_gu_b, w1_dn_b, 0, ffn_tile)
        pa, pr, lg = _proj_sample(x_all, sample_block, n_sample, l, norm_g, wa_b, wr_b, wl_b, wa2_b, b_a3)
        o2, sg_s, sr_s = _core_sample(pa, pr, lg, l, n_tok, cos_s, sin_s, gq_s, gk_s, state_gla, state_ret)
        x_all, sg_p, sr_p = _mixer_prompt(x_all, batch, seq, mix_tile, l, norm_g, wa_b, wr_b, wl_b, wa2_b,
                                          b_a3, gn, w_out_b, cos_p, sin_p, gq_p, gk_p)
        x_all = _out_sample(o2, pa, pr, l, gn, w_out_b, norm_g, x_all, sample_block)
        if l == DEPTH - 1:
            yp, ys = _ffn_split_out(x_all, n_prompt, l, norm_g, w2_gu_b, w2_dn_b, 4)
        else:
            x_all = _ffn(x_all, l, norm_g, w2_gu_b, w2_dn_b, 4, ffn_tile)
        gla_p.append(sg_p)
        ret_p.append(sr_p)
        gla_s.append(sg_s)
        ret_s.append(sr_s)

    return (yp.reshape(batch, seq, D_MODEL), ys.reshape(n_seq, n_tok, D_MODEL),
            jnp.stack(gla_p), jnp.stack(ret_p), jnp.stack(gla_s), jnp.stack(ret_s))
```

```python
import functools
import math

import jax
import jax.numpy as jnp
from jax import lax
from jax.experimental import pallas as pl
from jax.experimental.pallas import tpu as pltpu

F32 = jnp.float32
BF16 = jnp.bfloat16

D_MODEL = 1024
DEPTH = 4
PAST_LEN = 16384
N_HEADS = 4
DK_GLA = 64
DV = 128
GATE_RANK = 16
GATE_TEMP = 16.0
D_FF = 2816
ROPE_BASE = 10000.0
EPS = 1e-6
GLA_CHUNK = 64

A_COLS = 1536
R_COLS = 2048
QA, KA, VA, GA = 0, 256, 512, 1024
QR, KR, VR, GR = 0, 512, 1024, 1536
AL_PAD = 128

FFN_SUB_TILE = 512
FFN_FF_TILE = 256
MIX_SUB_TILE = 256
FFN_SUB_TILES_PER_STEP = 1
MIX_SUB_TILES_PER_STEP = 1
SAMPLE_SEQ_BLOCK = 16
VMEM_LIMIT_BYTES = 56 * 1024 * 1024
GLA_SAFE_GATE_ABS = 14.0
GLA_SAFE_KEY_ABS = 1e9

LOG_GAMMA = tuple(math.log1p(-(2.0 ** (-5.0 - h))) for h in range(N_HEADS))


def _mm(a, b):
    return jnp.dot(a, b, preferred_element_type=F32)


def _mm_nt(a, b):
    return lax.dot_general(a, b, (((1,), (1,)), ((), ())), preferred_element_type=F32)


def _mm_tn(a, b):
    return lax.dot_general(a, b, (((0,), (0,)), ((), ())), preferred_element_type=F32)


def _rms(x, g):
    ms = jnp.mean(x * x, axis=-1, keepdims=True)
    return x * lax.rsqrt(ms + EPS) * g


def _silu(x):
    return x * (1.0 / (1.0 + jnp.exp(-x)))


def _log_sigmoid(x):
    return jnp.minimum(x, 0.0) - jnp.log1p(jnp.exp(-jnp.abs(x)))


def _split_bf16(x):
    hi = x.astype(BF16)
    lo = (x - hi.astype(F32)).astype(BF16)
    return hi, lo


def _iota(shape, dim):
    return lax.broadcasted_iota(jnp.int32, shape, dim)


def _rotary(x, cos2, sin2):
    return x * cos2 + pltpu.roll(x, DV // 2, axis=1) * sin2


def _head_norm_gate(o, gn, gate):
    outs = []
    for h in range(N_HEADS):
        oh = o[:, h * DV:(h + 1) * DV]
        ms = jnp.mean(oh * oh, axis=-1, keepdims=True)
        outs.append(oh * lax.rsqrt(ms + EPS))
    return jnp.concatenate(outs, axis=1) * gn * _silu(gate)


def _first_of_block(x, block, row):
    bit = 1
    while bit < block:
        x = jnp.where(jnp.bitwise_and(row, bit) != 0, pltpu.roll(x, bit, axis=0), x)
        bit *= 2
    return x


def _stack_heads(x, k_head_mask_b):
    return jnp.concatenate([x.astype(BF16)] * N_HEADS, axis=0) * k_head_mask_b


def _gla_level_operands(qv, kv, bv, lgv, k_head_mask_b):
    n = qv.shape[0]
    row = _iota((n, 256), 0)
    pos16, pos4, pos1 = jnp.right_shift(row, 4), jnp.bitwise_and(jnp.right_shift(row, 2), 3), jnp.bitwise_and(row, 3)
    b_first16 = jnp.concatenate([jnp.broadcast_to(bv[16 * g:16 * g + 1, :], (16, 256))
                                 for g in range(n // 16)], axis=0)
    q1 = qv * jnp.exp(bv - b_first16)
    q2 = qv * jnp.exp(bv - _first_of_block(bv, 4, row))
    k1 = [jnp.where(row < 16 * j, kv * jnp.exp(bv[16 * j:16 * j + 1, :] - bv), 0.0) for j in range(1, 4)]
    k2 = []
    for j in range(1, 4):
        ref = jnp.concatenate([jnp.broadcast_to(bv[16 * g + 4 * j:16 * g + 4 * j + 1, :], (16, 256))
                               for g in range(n // 16)], axis=0)
        k2.append(jnp.where(jnp.bitwise_and(row, 15) < 4 * j, kv * jnp.exp(ref - bv), 0.0))
    f1 = pltpu.roll(lgv, n - 1, axis=0)
    f2 = f1 + pltpu.roll(lgv, n - 2, axis=0)
    f3 = f2 + pltpu.roll(lgv, n - 3, axis=0)
    ahead = [kv, kv * jnp.exp(f1), kv * jnp.exp(f2), kv * jnp.exp(f3)]
    k3 = []
    for j in range(4):
        k_j = jnp.zeros_like(kv)
        for p in range(j + 1):
            k_j = jnp.where(pos1 == p, ahead[j - p], k_j)
        k3.append(k_j)
    lhs = []
    for q_lvl, pos, first_slot in ((q1, pos16, 1), (q2, pos4, 1), (qv, pos1, 0)):
        q_b = q_lvl.astype(BF16)
        lhs += [q_b * jnp.where(pos == j, 1.0, 0.0).astype(BF16) for j in range(first_slot, 4)]
    rhs = [_stack_heads(k_j, k_head_mask_b) for k_j in k1 + k2 + k3]
    return jnp.concatenate(lhs, axis=1), jnp.concatenate(rhs, axis=1)


def _gla_level_scores(lhs, rhs):
    n, w = lhs.shape[0], N_HEADS * DK_GLA
    s1 = _mm_nt(lhs[:, 0:3 * w], rhs[:, 0:3 * w])
    s2 = _mm_nt(lhs[:, 3 * w:6 * w], rhs[:, 3 * w:6 * w])
    s3 = _mm_nt(lhs[:, 6 * w:10 * w], rhs[:, 6 * w:10 * w])
    t_idx = _iota((n, N_HEADS * n), 0)
    s_idx = jnp.bitwise_and(_iota((n, N_HEADS * n), 1), n - 1)
    same16 = jnp.right_shift(t_idx, 4) == jnp.right_shift(s_idx, 4)
    same4 = jnp.right_shift(t_idx, 2) == jnp.right_shift(s_idx, 2)
    return jnp.where(same4, s3, jnp.where(same16, s2, s1))


def _layer_spec(arr, layer, n_grid_axes):
    zeros = (0,) * (arr.ndim - 1)
    if n_grid_axes == 1:
        return pl.BlockSpec((None,) + arr.shape[1:], lambda i: (layer,) + zeros)
    return pl.BlockSpec((None,) + arr.shape[1:], lambda i, j: (layer,) + zeros)


def _params(n_axes):
    return pltpu.CompilerParams(dimension_semantics=("arbitrary",) * n_axes,
                                vmem_limit_bytes=VMEM_LIMIT_BYTES)


def _ffn_rows(x, ng_ref, wgu_ref, wdn_ref, pre_row):
    xn = _rms(x, ng_ref[pre_row:pre_row + 1, :]).astype(BF16)
    acc = jnp.zeros(x.shape, F32)
    for c in range(D_FF // FFN_FF_TILE):
        lo = c * FFN_FF_TILE
        gate = _mm(xn, wgu_ref[:, lo:lo + FFN_FF_TILE])
        up = _mm(xn, wgu_ref[:, D_FF + lo:D_FF + lo + FFN_FF_TILE])
        act = (_silu(gate) * up).astype(BF16)
        acc = acc + _mm(act, wdn_ref[lo:lo + FFN_FF_TILE, :])
    return x + 0.5 * _rms(acc, ng_ref[pre_row + 1:pre_row + 2, :])


def _ffn_kernel(x_ref, ng_ref, wgu_ref, wdn_ref, o_ref, *, pre_row):
    for s in range(x_ref.shape[0] // FFN_SUB_TILE):
        rs = slice(s * FFN_SUB_TILE, (s + 1) * FFN_SUB_TILE)
        o_ref[rs, :] = _ffn_rows(x_ref[rs, :], ng_ref, wgu_ref, wdn_ref, pre_row)


def _ffn_split_in_kernel(xp_ref, xs_ref, ng_ref, wgu_ref, wdn_ref, o_ref, *, pre_row, n_prompt_tiles):
    x = jnp.where(pl.program_id(0) < n_prompt_tiles, xp_ref[...], xs_ref[...])
    o_ref[...] = _ffn_rows(x, ng_ref, wgu_ref, wdn_ref, pre_row)


def _ffn_split_out_kernel(x_ref, ng_ref, wgu_ref, wdn_ref, op_ref, os_ref, *, pre_row, n_prompt_tiles):
    y = _ffn_rows(x_ref[...], ng_ref, wgu_ref, wdn_ref, pre_row)
    is_prompt = pl.program_id(0) < n_prompt_tiles

    @pl.when(is_prompt)
    def _():
        op_ref[...] = y

    @pl.when(jnp.logical_not(is_prompt))
    def _():
        os_ref[...] = y


def _ffn(x_all, layer, norm_g, wgu, wdn, pre_row, tile):
    return pl.pallas_call(
        functools.partial(_ffn_kernel, pre_row=pre_row),
        out_shape=jax.ShapeDtypeStruct(x_all.shape, F32),
        grid=(x_all.shape[0] // tile,),
        in_specs=[pl.BlockSpec((tile, D_MODEL), lambda i: (i, 0)),
                  _layer_spec(norm_g, layer, 1), _layer_spec(wgu, layer, 1), _layer_spec(wdn, layer, 1)],
        out_specs=pl.BlockSpec((tile, D_MODEL), lambda i: (i, 0)),
        compiler_params=_params(1),
        name="ffn",
    )(x_all, norm_g, wgu, wdn)


def _ffn_split_in(xp, xs, layer, norm_g, wgu, wdn, pre_row):
    tm = FFN_SUB_TILE
    n_p, n_s = xp.shape[0] // tm, xs.shape[0] // tm
    return pl.pallas_call(
        functools.partial(_ffn_split_in_kernel, pre_row=pre_row, n_prompt_tiles=n_p),
        out_shape=jax.ShapeDtypeStruct((xp.shape[0] + xs.shape[0], D_MODEL), F32),
        grid=(n_p + n_s,),
        in_specs=[pl.BlockSpec((tm, D_MODEL), lambda i: (jnp.minimum(i, n_p - 1), 0)),
                  pl.BlockSpec((tm, D_MODEL), lambda i: (jnp.maximum(i - n_p, 0), 0)),
                  _layer_spec(norm_g, layer, 1), _layer_spec(wgu, layer, 1), _layer_spec(wdn, layer, 1)],
        out_specs=pl.BlockSpec((tm, D_MODEL), lambda i: (i, 0)),
        compiler_params=_params(1),
        name="ffn_split_in",
    )(xp, xs, norm_g, wgu, wdn)


def _ffn_split_out(x_all, n_prompt, layer, norm_g, wgu, wdn, pre_row):
    tm = FFN_SUB_TILE
    n_p, n_s = n_prompt // tm, (x_all.shape[0] - n_prompt) // tm
    return pl.pallas_call(
        functools.partial(_ffn_split_out_kernel, pre_row=pre_row, n_prompt_tiles=n_p),
        out_shape=(jax.ShapeDtypeStruct((n_prompt, D_MODEL), F32),
                   jax.ShapeDtypeStruct((x_all.shape[0] - n_prompt, D_MODEL), F32)),
        grid=(n_p + n_s,),
        in_specs=[pl.BlockSpec((tm, D_MODEL), lambda i: (i, 0)),
                  _layer_spec(norm_g, layer, 1), _layer_spec(wgu, layer, 1), _layer_spec(wdn, layer, 1)],
        out_specs=(pl.BlockSpec((tm, D_MODEL), lambda i: (jnp.minimum(i, n_p - 1), 0)),
                   pl.BlockSpec((tm, D_MODEL), lambda i: (jnp.maximum(i - n_p, 0), 0))),
        compiler_params=_params(1),
        name="ffn_split_out",
    )(x_all, norm_g, wgu, wdn)


def _mixer_rows(x, cos2, sin2, sgt, sr, consts, robust, ng_ref, wa_ref, wr_ref, wl_ref, wa2_ref,
                ba_ref, gn_ref, wout_ref, gq_ref, gk_ref):
    tri, k_head_mask, v_head_mask, causal, bd_mask, causal_r = consts
    tt = x.shape[0]
    xn = _rms(x, ng_ref[2:3, :]).astype(BF16)

    alo = _mm(xn, wl_ref[...]).astype(BF16)
    gate_pre = _mm(alo, wa2_ref[...])
    q = _mm(xn, wa_ref[:, QA:QA + 256]) * (DK_GLA ** -0.5)
    k = _mm(xn, wa_ref[:, KA:KA + 256])
    v = _mm(xn, wa_ref[:, VA:VA + 512]).astype(BF16)
    lg = _log_sigmoid(gate_pre + ba_ref[...]) * (1.0 / GATE_TEMP)
    lg_hi, lg_lo = _split_bf16(lg)
    bcum = _mm(tri, lg_hi) + _mm(tri, lg_lo)
    qr = _mm(xn, wr_ref[:, QR:QR + 512])
    kr = _mm(xn, wr_ref[:, KR:KR + 512])
    b_end = bcum[tt - 1:tt, :]

    o_inter = _mm_nt((q * jnp.exp(bcum)).astype(BF16), sgt.astype(BF16))
    k_end = (k * jnp.exp(b_end - bcum)).astype(BF16)
    sgt = jnp.exp(b_end) * sgt + bd_mask * _mm_tn(v, k_end)
    vr = _mm(xn, wr_ref[:, VR:VR + 512]).astype(BF16)

    n_chunks = tt // GLA_CHUNK
    chunk = lambda j: slice(j * GLA_CHUNK, (j + 1) * GLA_CHUNK)
    chunk_last = lambda j: bcum[(j + 1) * GLA_CHUNK - 1:(j + 1) * GLA_CHUNK, :]
    vstacks = [jnp.where(v_head_mask, jnp.concatenate([v[chunk(j)]] * N_HEADS, axis=0), 0.0)
               for j in range(n_chunks)]

    def gla_cross_scores(cj):
        later = slice((cj + 1) * GLA_CHUNK, tt)
        q_later = (q[later] * jnp.exp(bcum[later] - chunk_last(cj))).astype(BF16)
        kc = k[chunk(cj)] * jnp.exp(chunk_last(cj) - bcum[chunk(cj)])
        return _mm_nt(q_later, _stack_heads(kc, k_head_mask))

    def gla_cross_values(cj, scores):
        return _mm(scores.astype(BF16), vstacks[cj])

    def gla_same_scores(ci):
        if robust:
            return _gla_level_scores(*_gla_level_operands(q[chunk(ci)], k[chunk(ci)], bcum[chunk(ci)],
                                                          lg[chunk(ci)], k_head_mask))
        ref = chunk_last(ci - 1) if ci > 0 else jnp.zeros((1, 256), F32)
        qc = (q[chunk(ci)] * jnp.exp(bcum[chunk(ci)] - ref)).astype(BF16)
        kc = k[chunk(ci)] * jnp.exp(ref - bcum[chunk(ci)])
        return jnp.where(causal, _mm_nt(qc, _stack_heads(kc, k_head_mask)), 0.0)

    def gla_same_values(ci, scores):
        return _mm(scores.astype(BF16), vstacks[ci])

    def ret_scores(h):
        hs = slice(h * DV, (h + 1) * DV)
        qh = (_rotary(qr[:, hs], cos2, sin2) * gq_ref[h]).astype(BF16)
        kh = (_rotary(kr[:, hs], cos2, sin2) * gk_ref[h]).astype(BF16)
        vh = vr[:, hs]
        a = _mm_nt(qh, kh)
        o_state = _mm(qh, sr[h].astype(BF16))
        s_new = math.exp(tt * LOG_GAMMA[h]) * (sr[h] + _mm_tn(kh, vh))
        return a, o_state, s_new, vh

    def ret_values(h, staged):
        a, o_state, _, vh = staged
        return o_state + _mm(jnp.where(causal_r, a, 0.0).astype(BF16), vh)

    stage1 = {"ret": ret_scores, "same": gla_same_scores, "cross": gla_cross_scores}
    stage2 = {"ret": ret_values, "same": gla_same_values, "cross": gla_cross_values}
    items = []
    for i in range(max(n_chunks, N_HEADS)):
        if i < N_HEADS:
            items.append(("ret", i))
        if i < n_chunks:
            items.append(("same", i))
        if i < n_chunks - 1:
            items.append(("cross", i))
    gate_cols = [(wa_ref, GA), (wa_ref, GA + 256), (wr_ref, GR), (wr_ref, GR + 256)]
    gate_parts, staged, done = [], {}, {}
    depth = 1
    for step in range(len(items) + depth):
        if step < len(items):
            kind, i = items[step]
            staged[(kind, i)] = stage1[kind](i)
        if step % 2 == 1 and len(gate_parts) < len(gate_cols):
            w_ref, off = gate_cols[len(gate_parts)]
            gate_parts.append(_mm(xn, w_ref[:, off:off + 256]))
        if step >= depth:
            kind, i = items[step - depth]
            done[(kind, i)] = stage2[kind](i, staged[(kind, i)])
    while len(gate_parts) < len(gate_cols):
        w_ref, off = gate_cols[len(gate_parts)]
        gate_parts.append(_mm(xn, w_ref[:, off:off + 256]))
    gate_a = jnp.concatenate(gate_parts[0:2], axis=1)
    gate_r = jnp.concatenate(gate_parts[2:4], axis=1)

    o_a = o_inter + jnp.concatenate([done[("same", ci)] for ci in range(n_chunks)], axis=0)
    for cj in range(n_chunks - 1):
        o_a = o_a + jnp.concatenate(
            [jnp.zeros(((cj + 1) * GLA_CHUNK, N_HEADS * DV), F32), done[("cross", cj)]], axis=0)
    o_r = jnp.concatenate([done[("ret", h)] for h in range(N_HEADS)], axis=1)
    sr_new = [staged[("ret", h)][2] for h in range(N_HEADS)]

    y_a = _head_norm_gate(o_a, gn_ref[:, 0:512], gate_a)
    y_r = _head_norm_gate(o_r, gn_ref[:, 512:1024], gate_r)
    y = jnp.concatenate([y_a, y_r], axis=1).astype(BF16)
    return x + _rms(_mm(y, wout_ref[...]), ng_ref[3:4, :]), sgt, sr_new


def _mixer_prompt_kernel(x_ref, ng_ref, wa_ref, wr_ref, wl_ref, wa2_ref, ba_ref, gn_ref, wout_ref,
                         cos_ref, sin_ref, gq_ref, gk_ref, o_ref, sg_ref, sr_ref, sgt_s, sr_s, *, robust):
    t_idx = pl.program_id(1)

    @pl.when(t_idx == 0)
    def _():
        sgt_s[...] = jnp.zeros(sgt_s.shape, F32)
        sr_s[...] = jnp.zeros(sr_s.shape, F32)

    sub = MIX_SUB_TILE
    rows4 = N_HEADS * GLA_CHUNK
    r = _iota((sub, sub), 0)
    c = _iota((sub, sub), 1)
    consts = (
        jnp.where(c <= r, 1.0, 0.0).astype(BF16),
        jnp.where(jnp.right_shift(_iota((rows4, 256), 0), 6)
                  == jnp.right_shift(_iota((rows4, 256), 1), 6), 1.0, 0.0).astype(BF16),
        jnp.right_shift(_iota((rows4, 512), 0), 6) == jnp.right_shift(_iota((rows4, 512), 1), 7),
        jnp.bitwise_and(_iota((GLA_CHUNK, rows4), 1), GLA_CHUNK - 1) <= _iota((GLA_CHUNK, rows4), 0),
        jnp.where(jnp.right_shift(_iota((512, 256), 0), 7)
                  == jnp.right_shift(_iota((512, 256), 1), 6), 1.0, 0.0),
        c <= r,
    )
    sgt = sgt_s[...]
    sr = [sr_s[h] for h in range(N_HEADS)]
    for s in range(x_ref.shape[0] // sub):
        rs = slice(s * sub, (s + 1) * sub)
        y, sgt, sr = _mixer_rows(x_ref[rs, :], cos_ref[rs, :], sin_ref[rs, :], sgt, sr, consts, robust,
                                 ng_ref, wa_ref, wr_ref, wl_ref, wa2_ref, ba_ref, gn_ref, wout_ref,
                                 gq_ref, gk_ref)
        o_ref[rs, :] = y
    sgt_s[...] = sgt
    for h in range(N_HEADS):
        sr_s[h] = sr[h]

    @pl.when(t_idx == pl.num_programs(1) - 1)
    def _():
        sg = sgt_s[...].T
        for h in range(N_HEADS):
            sg_ref[0, h] = sg[h * DK_GLA:(h + 1) * DK_GLA, h * DV:(h + 1) * DV]
        sr_ref[0] = sr_s[...]


def _mixer_prompt(x_all, robust, batch, seq, tt, layer, norm_g, wa, wr, wl, wa2, ba, gn, wout, cos2, sin2,
                  gq, gk):
    n_t = seq // tt
    x_map = lambda b, t: (b * n_t + t, 0)
    const3 = lambda b, t: (0, 0, 0)
    out_shapes = (
        jax.ShapeDtypeStruct(x_all.shape, F32),
        jax.ShapeDtypeStruct((batch, N_HEADS, DK_GLA, DV), F32),
        jax.ShapeDtypeStruct((batch, N_HEADS, DV, DV), F32),
    )
    weights = (norm_g, wa, wr, wl, wa2, ba, gn, wout)
    return pl.pallas_call(
        functools.partial(_mixer_prompt_kernel, robust=robust),
        out_shape=out_shapes,
        grid=(batch, n_t),
        in_specs=[pl.BlockSpec((tt, D_MODEL), x_map)]
        + [_layer_spec(w, layer, 2) for w in weights]
        + [pl.BlockSpec((tt, DV), lambda b, t: (t, 0)),
           pl.BlockSpec((tt, DV), lambda b, t: (t, 0)),
           pl.BlockSpec(gq.shape, const3),
           pl.BlockSpec(gk.shape, const3)],
        out_specs=(
            pl.BlockSpec((tt, D_MODEL), x_map),
            pl.BlockSpec((1, N_HEADS, DK_GLA, DV), lambda b, t: (b, 0, 0, 0)),
            pl.BlockSpec((1, N_HEADS, DV, DV), lambda b, t: (b, 0, 0, 0)),
        ),
        scratch_shapes=[
            pltpu.VMEM((N_HEADS * DV, N_HEADS * DK_GLA), F32),
            pltpu.VMEM((N_HEADS, DV, DV), F32),
        ],
        input_output_aliases={0: 0},
        compiler_params=_params(2),
        name="mixer_prompt_levels" if robust else "mixer_prompt",
    )(x_all, *weights, cos2, sin2, gq, gk)


def _gate_range_kernel(x_ref, ng_ref, wl_ref, wa2_ref, ba_ref, o_ref):
    xn = _rms(x_ref[...], ng_ref[2:3, :]).astype(BF16)
    alo = _mm(xn, wl_ref[...]).astype(BF16)
    g = jnp.max(jnp.abs(_mm(alo, wa2_ref[...]) + ba_ref[...]), axis=0, keepdims=True)
    o_ref[0] = jnp.broadcast_to(jnp.maximum(g[:, 0:128], g[:, 128:256]), (8, 128))


def _gate_range(x_all, n_prompt, layer, norm_g, wl, wa2, ba):
    tm = FFN_SUB_TILE
    weights = (norm_g, wl, wa2, ba)
    return pl.pallas_call(
        _gate_range_kernel,
        out_shape=jax.ShapeDtypeStruct((n_prompt // tm, 8, 128), F32),
        grid=(n_prompt // tm,),
        in_specs=[pl.BlockSpec((tm, D_MODEL), lambda i: (i, 0))] + [_layer_spec(w, layer, 1) for w in weights],
        out_specs=pl.BlockSpec((1, 8, 128), lambda i: (i, 0, 0)),
        compiler_params=_params(1),
        name="gate_range",
    )(x_all, *weights)


def _proj_sample_kernel(x_ref, ng_ref, wa_ref, wr_ref, wl_ref, wa2_ref, ba_ref, pa_ref, pr_ref, lg_ref):
    xn = _rms(x_ref[...], ng_ref[2:3, :]).astype(BF16)
    pa_ref[...] = _mm(xn, wa_ref[...])
    pr_ref[...] = _mm(xn, wr_ref[...])
    alo = _mm(xn, wl_ref[...]).astype(BF16)
    lg_ref[...] = _log_sigmoid(_mm(alo, wa2_ref[...]) + ba_ref[...]) * (1.0 / GATE_TEMP)


def _proj_sample(x_all, row_block, n_rows, layer, norm_g, wa, wr, wl, wa2, ba):
    const = lambda i: (0, 0)
    weights = (norm_g, wa, wr, wl, wa2, ba)
    return pl.pallas_call(
        _proj_sample_kernel,
        out_shape=(jax.ShapeDtypeStruct((n_rows, A_COLS), F32),
                   jax.ShapeDtypeStruct((n_rows, R_COLS), F32),
                   jax.ShapeDtypeStruct((n_rows, 256), F32)),
        grid=(1,),
        in_specs=[pl.BlockSpec((n_rows, D_MODEL), lambda i: (row_block, 0))]
        + [_layer_spec(w, layer, 1) for w in weights],
        out_specs=(pl.BlockSpec((n_rows, A_COLS), const),
                   pl.BlockSpec((n_rows, R_COLS), const),
                   pl.BlockSpec((n_rows, 256), const)),
        compiler_params=_params(1),
        name="proj_sample",
    )(x_all, *weights)


def _core_sample_kernel(pa_ref, pr_ref, lg_ref, cos_ref, sin_ref, gq_ref, gk_ref, sg_in, sr_in,
                        o_ref, sg_out, sr_out, *, n_tok):
    n_seq = sg_in.shape[0]
    rows4 = N_HEADS * n_tok
    tok_shift = n_tok.bit_length() - 1
    k_head_mask = jnp.where(jnp.right_shift(_iota((rows4, 256), 0), tok_shift)
                            == jnp.right_shift(_iota((rows4, 256), 1), 6), 1.0, 0.0).astype(BF16)
    v_head_mask = (jnp.right_shift(_iota((rows4, 512), 0), tok_shift)
                   == jnp.right_shift(_iota((rows4, 512), 1), 7))
    tok_row = _iota((n_tok, 256), 0)
    causal_r = _iota((n_tok, n_tok), 1) <= _iota((n_tok, n_tok), 0)
    ones = jnp.ones((2 * n_tok, DV), BF16)
    cos2 = cos_ref[...]
    sin2 = sin_ref[...]

    def one_seq(n, pa, pr, lg):
        b_rows = [lg[0:1]]
        for t in range(1, n_tok):
            b_rows.append(b_rows[-1] + lg[t:t + 1])
        bc = jnp.concatenate(b_rows, axis=0)
        b_last = b_rows[-1]
        k = pa[:, KA:KA + 256]
        v = pa[:, VA:VA + 512].astype(BF16)
        q = pa[:, QA:QA + 256] * (DK_GLA ** -0.5)
        qt = (q * jnp.exp(bc)).astype(BF16)
        kp = (k * jnp.exp(b_last - bc)).astype(BF16)
        s_old = sg_in[n].reshape(N_HEADS * DK_GLA, DV)
        s_old_b = s_old.astype(BF16)
        s_bd = jnp.concatenate(
            [jnp.where(jnp.right_shift(_iota((N_HEADS * DK_GLA, DV), 0), 6) == h, s_old_b, 0.0)
             for h in range(N_HEADS)], axis=1)
        vstack = jnp.where(v_head_mask, jnp.concatenate([v] * N_HEADS, axis=0), 0.0)
        lhs = [jnp.where(tok_row == j, q, 0.0).astype(BF16) for j in range(n_tok)]
        rhs = [_stack_heads(jnp.where(tok_row <= j, k * jnp.exp(b_rows[j] - bc), 0.0), k_head_mask)
               for j in range(n_tok)]
        a = _mm_nt(jnp.concatenate(lhs, axis=1), jnp.concatenate(rhs, axis=1)).astype(BF16)
        o_a = _mm(qt, s_bd) + _mm(a, vstack)
        lg_hi, lg_lo = _split_bf16(lg)
        b_last_t = _mm_tn(jnp.concatenate([lg_hi, lg_lo], axis=0), ones)
        u = _mm_tn(kp, v)
        u_d = jnp.concatenate(
            [u[h * DK_GLA:(h + 1) * DK_GLA, h * DV:(h + 1) * DV] for h in range(N_HEADS)], axis=0)
        sg_out[n] = (jnp.exp(b_last_t) * s_old + u_d).reshape(N_HEADS, DK_GLA, DV)
        vr = pr[:, VR:VR + 512].astype(BF16)
        o_r_heads = []
        for h in range(N_HEADS):
            hs = slice(h * DV, (h + 1) * DV)
            qh = (_rotary(pr[:, QR + h * DV:QR + (h + 1) * DV], cos2, sin2) * gq_ref[h]).astype(BF16)
            kh = (_rotary(pr[:, KR + h * DV:KR + (h + 1) * DV], cos2, sin2) * gk_ref[h]).astype(BF16)
            vh = vr[:, hs]
            ar = jnp.where(causal_r, _mm_nt(qh, kh), 0.0).astype(BF16)
            s_r = sr_in[n, h]
            o_r_heads.append(_mm(qh, s_r.astype(BF16)) + _mm(ar, vh))
            sr_out[n, h] = math.exp(n_tok * LOG_GAMMA[h]) * (s_r + _mm_tn(kh, vh))
        return jnp.concatenate([o_a] + o_r_heads, axis=1)

    seq_per_iter = 8 // n_tok

    def body(i, carry):
        rows = pl.ds(pl.multiple_of(i * 8, 8), 8)
        pa8, pr8, lg8 = pa_ref[rows, :], pr_ref[rows, :], lg_ref[rows, :]
        outs = []
        for j in range(seq_per_iter):
            sl = slice(j * n_tok, (j + 1) * n_tok)
            outs.append(one_seq(i * seq_per_iter + j, pa8[sl], pr8[sl], lg8[sl]))
        o_ref[rows, :] = jnp.concatenate(outs, axis=0)
        return carry

    lax.fori_loop(0, n_seq // seq_per_iter, body, 0)


def _core_sample(pa, pr, lg, layer, n_tok, cos2, sin2, gq, gk, state_gla, state_ret):
    n_seq = state_gla.shape[1]
    nb = SAMPLE_SEQ_BLOCK
    rows = nb * n_tok
    const2 = lambda i: (0, 0)
    const3 = lambda i: (0, 0, 0)
    out_shapes = (
        jax.ShapeDtypeStruct((n_seq * n_tok, D_MODEL), F32),
        jax.ShapeDtypeStruct(state_gla.shape[1:], F32),
        jax.ShapeDtypeStruct(state_ret.shape[1:], F32),
    )
    return pl.pallas_call(
        functools.partial(_core_sample_kernel, n_tok=n_tok),
        out_shape=out_shapes,
        grid=(n_seq // nb,),
        in_specs=[
            pl.BlockSpec((rows, A_COLS), lambda i: (i, 0)),
            pl.BlockSpec((rows, R_COLS), lambda i: (i, 0)),
            pl.BlockSpec((rows, 256), lambda i: (i, 0)),
            pl.BlockSpec(cos2.shape, const2),
            pl.BlockSpec(sin2.shape, const2),
            pl.BlockSpec(gq.shape, const3),
            pl.BlockSpec(gk.shape, const3),
            pl.BlockSpec((None, nb, N_HEADS, DK_GLA, DV), lambda i: (layer, i, 0, 0, 0)),
            pl.BlockSpec((None, nb, N_HEADS, DV, DV), lambda i: (layer, i, 0, 0, 0)),
        ],
        out_specs=(
            pl.BlockSpec((rows, D_MODEL), lambda i: (i, 0)),
            pl.BlockSpec((nb, N_HEADS, DK_GLA, DV), lambda i: (i, 0, 0, 0)),
            pl.BlockSpec((nb, N_HEADS, DV, DV), lambda i: (i, 0, 0, 0)),
        ),
        compiler_params=_params(1),
        name="core_sample",
    )(pa, pr, lg, cos2, sin2, gq, gk, state_gla, state_ret)


def _out_sample_kernel(o_ref, ga_ref, gr_ref, gn_ref, wout_ref, ng_ref, x_ref, y_ref):
    o = o_ref[...]
    y_a = _head_norm_gate(o[:, 0:512], gn_ref[:, 0:512], ga_ref[...])
    y_r = _head_norm_gate(o[:, 512:1024], gn_ref[:, 512:1024], gr_ref[...])
    y = jnp.concatenate([y_a, y_r], axis=1).astype(BF16)
    y_ref[...] = x_ref[...] + _rms(_mm(y, wout_ref[...]), ng_ref[3:4, :])


def _out_sample(o2, pa, pr, layer, gn, wout, norm_g, x_all, row_block):
    n_rows = o2.shape[0]
    const = lambda i: (0, 0)
    return pl.pallas_call(
        _out_sample_kernel,
        out_shape=jax.ShapeDtypeStruct(x_all.shape, F32),
        grid=(1,),
        in_specs=[
            pl.BlockSpec(o2.shape, const),
            pl.BlockSpec((n_rows, 512), lambda i: (0, GA // 512)),
            pl.BlockSpec((n_rows, 512), lambda i: (0, GR // 512)),
            _layer_spec(gn, layer, 1),
            _layer_spec(wout, layer, 1),
            _layer_spec(norm_g, layer, 1),
            pl.BlockSpec((n_rows, D_MODEL), lambda i: (row_block, 0)),
        ],
        out_specs=pl.BlockSpec((n_rows, D_MODEL), lambda i: (row_block, 0)),
        input_output_aliases={6: 0},
        compiler_params=_params(1),
        name="out_sample",
    )(o2, pa, pr, gn, wout, norm_g, x_all)


def _rope_tables(pos):
    half = DV // 2
    inv_freq = ROPE_BASE ** (-jnp.arange(half, dtype=F32) / half)
    ang = pos[:, None] * inv_freq[None, :]
    cos, sin = jnp.cos(ang), jnp.sin(ang)
    return jnp.concatenate([cos, cos], axis=-1), jnp.concatenate([-sin, sin], axis=-1)


def _decay_tables(chunk):
    steps = (jnp.arange(chunk, dtype=F32) + 1.0)[None, :, None]
    lgam = jnp.asarray(LOG_GAMMA, F32)[:, None, None]
    gq = jnp.broadcast_to(jnp.exp(steps * lgam), (N_HEADS, chunk, DV))
    gk = jnp.broadcast_to(jnp.exp(-steps * lgam) * (DV ** -0.5), (N_HEADS, chunk, DV))
    return gq, gk


def _largest_tile(n_rows, sub, max_subs):
    return max(m * sub for m in range(1, max_subs + 1) if n_rows % (m * sub) == 0)


def kernel(x_prompt, x_sample, state_gla, state_ret, norm_g, w1_gu, w1_down, w2_gu, w2_down,
           w_in, w_a2, b_a, gn_gla, gn_ret, w_out):
    batch, seq, _ = x_prompt.shape
    n_seq, n_tok, _ = x_sample.shape
    n_prompt = batch * seq
    n_sample = n_seq * n_tok
    n_all = n_prompt + n_sample
    assert n_prompt % FFN_SUB_TILE == 0 and n_sample % FFN_SUB_TILE == 0
    assert seq % MIX_SUB_TILE == 0 and n_prompt % n_sample == 0
    assert n_seq % SAMPLE_SEQ_BLOCK == 0 and n_tok in (1, 2, 4, 8)
    ffn_tile = _largest_tile(n_all, FFN_SUB_TILE, FFN_SUB_TILES_PER_STEP)
    mix_tile = _largest_tile(seq, MIX_SUB_TILE, MIX_SUB_TILES_PER_STEP)
    sample_block = n_prompt // n_sample

    w1_gu_b, w1_dn_b = w1_gu.astype(BF16), w1_down.astype(BF16)
    w2_gu_b, w2_dn_b = w2_gu.astype(BF16), w2_down.astype(BF16)
    wa_b = w_in[..., :A_COLS].astype(BF16)
    wr_b = w_in[..., A_COLS + GATE_RANK:].astype(BF16)
    wl_b = jnp.pad(w_in[..., A_COLS:A_COLS + GATE_RANK].astype(BF16),
                   ((0, 0), (0, 0), (0, AL_PAD - GATE_RANK)))
    wa2_b = jnp.pad(w_a2.astype(BF16), ((0, 0), (0, AL_PAD - GATE_RANK), (0, 0)))
    w_out_b = w_out.astype(BF16)
    b_a3 = b_a[:, None, :]
    gn = jnp.concatenate([gn_gla, gn_ret], axis=-1)[:, None, :]

    key_bound = (math.sqrt(D_MODEL) * jnp.max(jnp.abs(norm_g[:, 2, :]), axis=-1)
                 * jnp.max(jnp.sqrt(jnp.sum(jnp.square(w_in[:, :, KA:KA + 256]), axis=1)), axis=-1))

    cos_p, sin_p = _rope_tables(jnp.arange(seq, dtype=F32))
    cos_s, sin_s = _rope_tables(PAST_LEN + jnp.arange(n_tok, dtype=F32))
    gq_p, gk_p = _decay_tables(MIX_SUB_TILE)
    gq_s, gk_s = _decay_tables(n_tok)

    xp = x_prompt.reshape(n_prompt, D_MODEL)
    xs = x_sample.reshape(n_sample, D_MODEL)
    gla_p, ret_p, gla_s, ret_s = [], [], [], []
    for l in range(DEPTH):
        if l == 0:
            x_all = _ffn_split_in(xp, xs, l, norm_g, w1_gu_b, w1_dn_b, 0)
        else:
            x_all = _ffn(x_all, l, norm_g, w1_gu_b, w1_dn_b, 0, ffn_tile)
        pa, pr, lg = _proj_sample(x_all, sample_block, n_sample, l, norm_g, wa_b, wr_b, wl_b, wa2_b, b_a3)
        o2, sg_s, sr_s = _core_sample(pa, pr, lg, l, n_tok, cos_s, sin_s, gq_s, gk_s, state_gla, state_ret)
        gate_max = jnp.max(_gate_range(x_all, n_prompt, l, norm_g, wl_b, wa2_b, b_a3))
        bounded = jnp.logical_and(gate_max < GLA_SAFE_GATE_ABS, key_bound[l] < GLA_SAFE_KEY_ABS)
        mixer = functools.partial(_mixer_prompt, batch=batch, seq=seq, tt=mix_tile, layer=l, norm_g=norm_g,
                                  wa=wa_b, wr=wr_b, wl=wl_b, wa2=wa2_b, ba=b_a3, gn=gn, wout=w_out_b,
                                  cos2=cos_p, sin2=sin_p, gq=gq_p, gk=gk_p)
        x_all, sg_p, sr_p = lax.cond(bounded, functools.partial(mixer, robust=False),
                                     functools.partial(mixer, robust=True), x_all)
        x_all = _out_sample(o2, pa, pr, l, gn, w_out_b, norm_g, x_all, sample_block)
        if l == DEPTH - 1:
            yp, ys = _ffn_split_out(x_all, n_prompt, l, norm_g, w2_gu_b, w2_dn_b, 4)
        else:
            x_all = _ffn(x_all, l, norm_g, w2_gu_b, w2_dn_b, 4, ffn_tile)
        gla_p.append(sg_p)
        ret_p.append(sr_p)
        gla_s.append(sg_s)
        ret_s.append(sr_s)

    return (yp.reshape(batch, seq, D_MODEL), ys.reshape(n_seq, n_tok, D_MODEL),
            jnp.stack(gla_p), jnp.stack(ret_p), jnp.stack(gla_s), jnp.stack(ret_s))
```

```python
import functools
import math

import jax
import jax.numpy as jnp
from jax import lax
from jax.experimental import pallas as pl
from jax.experimental.pallas import tpu as pltpu

F32 = jnp.float32
BF16 = jnp.bfloat16

D_MODEL = 1024
DEPTH = 4
PAST_LEN = 16384
N_HEADS = 4
DK_GLA = 64
DV = 128
GATE_RANK = 16
GATE_TEMP = 16.0
D_FF = 2816
ROPE_BASE = 10000.0
EPS = 1e-6
GLA_CHUNK = 64

A_COLS = 1536
R_COLS = 2048
QA, KA, VA, GA = 0, 256, 512, 1024
QR, KR, VR, GR = 0, 512, 1024, 1536
AL_PAD = 128

FFN_SUB_TILE = 512
FFN_FF_TILE = 256
MIX_SUB_TILE = 256
MIX_SUB_TILES_PER_STEP = 1
SAMPLE_SEQ_BLOCK = 16
VMEM_LIMIT_BYTES = 56 * 1024 * 1024
GLA_SAFE_GATE_ABS = 14.0
GLA_SAFE_KEY_ABS = 1e9

LOG_GAMMA = tuple(math.log1p(-(2.0 ** (-5.0 - h))) for h in range(N_HEADS))


def _mm(a, b):
    return jnp.dot(a, b, preferred_element_type=F32)


def _mm_nt(a, b):
    return lax.dot_general(a, b, (((1,), (1,)), ((), ())), preferred_element_type=F32)


def _mm_tn(a, b):
    return lax.dot_general(a, b, (((0,), (0,)), ((), ())), preferred_element_type=F32)


def _rms(x, g):
    ms = jnp.mean(x * x, axis=-1, keepdims=True)
    return x * lax.rsqrt(ms + EPS) * g


def _silu(x):
    return x * (1.0 / (1.0 + jnp.exp(-x)))


def _log_sigmoid(x):
    return jnp.minimum(x, 0.0) - jnp.log1p(jnp.exp(-jnp.abs(x)))


def _split_bf16(x):
    hi = x.astype(BF16)
    lo = (x - hi.astype(F32)).astype(BF16)
    return hi, lo


def _iota(shape, dim):
    return lax.broadcasted_iota(jnp.int32, shape, dim)


def _rotary(x, cos2, sin2):
    return x * cos2 + pltpu.roll(x, DV // 2, axis=1) * sin2


def _head_norm_gate(o, gn, gate):
    outs = []
    for h in range(N_HEADS):
        oh = o[:, h * DV:(h + 1) * DV]
        ms = jnp.mean(oh * oh, axis=-1, keepdims=True)
        outs.append(oh * lax.rsqrt(ms + EPS))
    return jnp.concatenate(outs, axis=1) * gn * _silu(gate)


def _first_of_block(x, block, row):
    bit = 1
    while bit < block:
        x = jnp.where(jnp.bitwise_and(row, bit) != 0, pltpu.roll(x, bit, axis=0), x)
        bit *= 2
    return x


def _stack_heads(x, k_head_mask_b):
    return jnp.concatenate([x.astype(BF16)] * N_HEADS, axis=0) * k_head_mask_b


def _gla_level_operands(qv, kv, bv, lgv, k_head_mask_b):
    n = qv.shape[0]
    row = _iota((n, 256), 0)
    pos16, pos4, pos1 = jnp.right_shift(row, 4), jnp.bitwise_and(jnp.right_shift(row, 2), 3), jnp.bitwise_and(row, 3)
    b_first16 = jnp.concatenate([jnp.broadcast_to(bv[16 * g:16 * g + 1, :], (16, 256))
                                 for g in range(n // 16)], axis=0)
    q1 = qv * jnp.exp(bv - b_first16)
    q2 = qv * jnp.exp(bv - _first_of_block(bv, 4, row))
    k1 = [jnp.where(row < 16 * j, kv * jnp.exp(bv[16 * j:16 * j + 1, :] - bv), 0.0) for j in range(1, 4)]
    k2 = []
    for j in range(1, 4):
        ref = jnp.concatenate([jnp.broadcast_to(bv[16 * g + 4 * j:16 * g + 4 * j + 1, :], (16, 256))
                               for g in range(n // 16)], axis=0)
        k2.append(jnp.where(jnp.bitwise_and(row, 15) < 4 * j, kv * jnp.exp(ref - bv), 0.0))
    f1 = pltpu.roll(lgv, n - 1, axis=0)
    f2 = f1 + pltpu.roll(lgv, n - 2, axis=0)
    f3 = f2 + pltpu.roll(lgv, n - 3, axis=0)
    ahead = [kv, kv * jnp.exp(f1), kv * jnp.exp(f2), kv * jnp.exp(f3)]
    k3 = []
    for j in range(4):
        k_j = jnp.zeros_like(kv)
        for p in range(j + 1):
            k_j = jnp.where(pos1 == p, ahead[j - p], k_j)
        k3.append(k_j)
    lhs = []
    for q_lvl, pos, first_slot in ((q1, pos16, 1), (q2, pos4, 1), (qv, pos1, 0)):
        q_b = q_lvl.astype(BF16)
        lhs += [q_b * jnp.where(pos == j, 1.0, 0.0).astype(BF16) for j in range(first_slot, 4)]
    rhs = [_stack_heads(k_j, k_head_mask_b) for k_j in k1 + k2 + k3]
    return jnp.concatenate(lhs, axis=1), jnp.concatenate(rhs, axis=1)


def _gla_level_scores(lhs, rhs):
    n, w = lhs.shape[0], N_HEADS * DK_GLA
    s1 = _mm_nt(lhs[:, 0:3 * w], rhs[:, 0:3 * w])
    s2 = _mm_nt(lhs[:, 3 * w:6 * w], rhs[:, 3 * w:6 * w])
    s3 = _mm_nt(lhs[:, 6 * w:10 * w], rhs[:, 6 * w:10 * w])
    t_idx = _iota((n, N_HEADS * n), 0)
    s_idx = jnp.bitwise_and(_iota((n, N_HEADS * n), 1), n - 1)
    same16 = jnp.right_shift(t_idx, 4) == jnp.right_shift(s_idx, 4)
    same4 = jnp.right_shift(t_idx, 2) == jnp.right_shift(s_idx, 2)
    return jnp.where(same4, s3, jnp.where(same16, s2, s1))


def _layer_spec(arr, layer, n_grid_axes):
    zeros = (0,) * (arr.ndim - 1)
    if n_grid_axes == 1:
        return pl.BlockSpec((None,) + arr.shape[1:], lambda i: (layer,) + zeros)
    return pl.BlockSpec((None,) + arr.shape[1:], lambda i, j: (layer,) + zeros)


def _params(n_axes):
    return pltpu.CompilerParams(dimension_semantics=("arbitrary",) * n_axes,
                                vmem_limit_bytes=VMEM_LIMIT_BYTES)


def _ffn_rows(x, ng_ref, wgu_ref, wdn_ref, pre_row):
    xn = _rms(x, ng_ref[pre_row:pre_row + 1, :]).astype(BF16)
    acc = jnp.zeros(x.shape, F32)
    for c in range(D_FF // FFN_FF_TILE):
        lo = c * FFN_FF_TILE
        gate = _mm(xn, wgu_ref[:, lo:lo + FFN_FF_TILE])
        up = _mm(xn, wgu_ref[:, D_FF + lo:D_FF + lo + FFN_FF_TILE])
        act = (_silu(gate) * up).astype(BF16)
        acc = acc + _mm(act, wdn_ref[lo:lo + FFN_FF_TILE, :])
    return x + 0.5 * _rms(acc, ng_ref[pre_row + 1:pre_row + 2, :])


def _gate_range_rows(x, ng_ref, wl_ref, wa2_ref, ba_ref):
    xn = _rms(x, ng_ref[2:3, :]).astype(BF16)
    alo = _mm(xn, wl_ref[...]).astype(BF16)
    g = jnp.max(jnp.abs(_mm(alo, wa2_ref[...]) + ba_ref[...]), axis=0, keepdims=True)
    return jnp.broadcast_to(jnp.maximum(g[:, 0:128], g[:, 128:256]), (8, 128))


def _ffn_kernel(*refs, pre_row, split_in, split_out, gate_range, n_prompt_tiles):
    refs = list(refs)
    x_refs = [refs.pop(0) for _ in range(2 if split_in else 1)]
    ng_ref, wgu_ref, wdn_ref = refs.pop(0), refs.pop(0), refs.pop(0)
    gate_refs = [refs.pop(0) for _ in range(3 if gate_range else 0)]
    y_refs = [refs.pop(0) for _ in range(2 if split_out else 1)]
    is_prompt = pl.program_id(0) < n_prompt_tiles
    x = jnp.where(is_prompt, x_refs[0][...], x_refs[1][...]) if split_in else x_refs[0][...]
    y = _ffn_rows(x, ng_ref, wgu_ref, wdn_ref, pre_row)
    if gate_range:
        refs.pop(0)[0] = _gate_range_rows(y, ng_ref, *gate_refs)
    if split_out:
        @pl.when(is_prompt)
        def _():
            y_refs[0][...] = y

        @pl.when(jnp.logical_not(is_prompt))
        def _():
            y_refs[1][...] = y
    else:
        y_refs[0][...] = y


def _ffn(xs, n_prompt, layer, norm_g, wgu, wdn, pre_row, split_out=False, gate_weights=None):
    tm = FFN_SUB_TILE
    split_in = isinstance(xs, tuple)
    n_all = sum(x.shape[0] for x in xs) if split_in else xs.shape[0]
    n_p, n_t = n_prompt // tm, n_all // tm
    joint = pl.BlockSpec((tm, D_MODEL), lambda i: (i, 0))
    split = [pl.BlockSpec((tm, D_MODEL), lambda i: (jnp.minimum(i, n_p - 1), 0)),
             pl.BlockSpec((tm, D_MODEL), lambda i: (jnp.maximum(i - n_p, 0), 0))]
    split_shapes = [jax.ShapeDtypeStruct((n_prompt, D_MODEL), F32),
                    jax.ShapeDtypeStruct((n_all - n_prompt, D_MODEL), F32)]
    weights = (norm_g, wgu, wdn) + tuple(gate_weights or ())
    out_shape = split_shapes if split_out else [jax.ShapeDtypeStruct((n_all, D_MODEL), F32)]
    out_specs = split if split_out else [joint]
    if gate_weights:
        out_shape = out_shape + [jax.ShapeDtypeStruct((n_t, 8, 128), F32)]
        out_specs = out_specs + [pl.BlockSpec((1, 8, 128), lambda i: (i, 0, 0))]
    outs = pl.pallas_call(
        functools.partial(_ffn_kernel, pre_row=pre_row, split_in=split_in, split_out=split_out,
                          gate_range=bool(gate_weights), n_prompt_tiles=n_p),
        out_shape=out_shape,
        grid=(n_t,),
        in_specs=(split if split_in else [joint]) + [_layer_spec(w, layer, 1) for w in weights],
        out_specs=out_specs,
        compiler_params=_params(1),
        name="ffn",
    )(*(xs if split_in else (xs,)), *weights)
    return outs if len(outs) > 1 else outs[0]


def _mixer_rows(x, cos2, sin2, sgt, sr, consts, robust, ng_ref, wa_ref, wr_ref, wl_ref, wa2_ref,
                ba_ref, gn_ref, wout_ref, gq_ref, gk_ref):
    tri, k_head_mask, v_head_mask, causal, bd_mask, causal_r = consts
    tt = x.shape[0]
    xn = _rms(x, ng_ref[2:3, :]).astype(BF16)

    alo = _mm(xn, wl_ref[...]).astype(BF16)
    gate_pre = _mm(alo, wa2_ref[...])
    q = _mm(xn, wa_ref[:, QA:QA + 256]) * (DK_GLA ** -0.5)
    k = _mm(xn, wa_ref[:, KA:KA + 256])
    v = _mm(xn, wa_ref[:, VA:VA + 512]).astype(BF16)
    lg = _log_sigmoid(gate_pre + ba_ref[...]) * (1.0 / GATE_TEMP)
    lg_hi, lg_lo = _split_bf16(lg)
    bcum = _mm(tri, lg_hi) + _mm(tri, lg_lo)
    qr = _mm(xn, wr_ref[:, QR:QR + 512])
    kr = _mm(xn, wr_ref[:, KR:KR + 512])
    b_end = bcum[tt - 1:tt, :]

    o_inter = _mm_nt((q * jnp.exp(bcum)).astype(BF16), sgt.astype(BF16))
    k_end = (k * jnp.exp(b_end - bcum)).astype(BF16)
    sgt = jnp.exp(b_end) * sgt + bd_mask * _mm_tn(v, k_end)
    vr = _mm(xn, wr_ref[:, VR:VR + 512]).astype(BF16)

    n_chunks = tt // GLA_CHUNK
    chunk = lambda j: slice(j * GLA_CHUNK, (j + 1) * GLA_CHUNK)
    chunk_last = lambda j: bcum[(j + 1) * GLA_CHUNK - 1:(j + 1) * GLA_CHUNK, :]
    vstacks = [jnp.where(v_head_mask, jnp.concatenate([v[chunk(j)]] * N_HEADS, axis=0), 0.0)
               for j in range(n_chunks)]

    def gla_cross_scores(cj):
        later = slice((cj + 1) * GLA_CHUNK, tt)
        q_later = (q[later] * jnp.exp(bcum[later] - chunk_last(cj))).astype(BF16)
        kc = k[chunk(cj)] * jnp.exp(chunk_last(cj) - bcum[chunk(cj)])
        return _mm_nt(q_later, _stack_heads(kc, k_head_mask))

    def gla_cross_values(cj, scores):
        return _mm(scores.astype(BF16), vstacks[cj])

    def gla_same_scores(ci):
        if robust:
            return _gla_level_scores(*_gla_level_operands(q[chunk(ci)], k[chunk(ci)], bcum[chunk(ci)],
                                                          lg[chunk(ci)], k_head_mask))
        ref = chunk_last(ci - 1) if ci > 0 else jnp.zeros((1, 256), F32)
        qc = (q[chunk(ci)] * jnp.exp(bcum[chunk(ci)] - ref)).astype(BF16)
        kc = k[chunk(ci)] * jnp.exp(ref - bcum[chunk(ci)])
        return jnp.where(causal, _mm_nt(qc, _stack_heads(kc, k_head_mask)), 0.0)

    def gla_same_values(ci, scores):
        return _mm(scores.astype(BF16), vstacks[ci])

    def ret_scores(h):
        hs = slice(h * DV, (h + 1) * DV)
        qh = (_rotary(qr[:, hs], cos2, sin2) * gq_ref[h]).astype(BF16)
        kh = (_rotary(kr[:, hs], cos2, sin2) * gk_ref[h]).astype(BF16)
        vh = vr[:, hs]
        a = _mm_nt(qh, kh)
        o_state = _mm(qh, sr[h].astype(BF16))
        s_new = math.exp(tt * LOG_GAMMA[h]) * (sr[h] + _mm_tn(kh, vh))
        return a, o_state, s_new, vh

    def ret_values(h, staged):
        a, o_state, _, vh = staged
        return o_state + _mm(jnp.where(causal_r, a, 0.0).astype(BF16), vh)

    stage1 = {"ret": ret_scores, "same": gla_same_scores, "cross": gla_cross_scores}
    stage2 = {"ret": ret_values, "same": gla_same_values, "cross": gla_cross_values}
    items = []
    for i in range(max(n_chunks, N_HEADS)):
        if i < N_HEADS:
            items.append(("ret", i))
        if i < n_chunks:
            items.append(("same", i))
        if i < n_chunks - 1:
            items.append(("cross", i))
    gate_cols = [(wa_ref, GA), (wa_ref, GA + 256), (wr_ref, GR), (wr_ref, GR + 256)]
    gate_parts, staged, done = [], {}, {}
    depth = 1
    for step in range(len(items) + depth):
        if step < len(items):
            kind, i = items[step]
            staged[(kind, i)] = stage1[kind](i)
        if step % 2 == 1 and len(gate_parts) < len(gate_cols):
            w_ref, off = gate_cols[len(gate_parts)]
            gate_parts.append(_mm(xn, w_ref[:, off:off + 256]))
        if step >= depth:
            kind, i = items[step - depth]
            done[(kind, i)] = stage2[kind](i, staged[(kind, i)])
    while len(gate_parts) < len(gate_cols):
        w_ref, off = gate_cols[len(gate_parts)]
        gate_parts.append(_mm(xn, w_ref[:, off:off + 256]))
    gate_a = jnp.concatenate(gate_parts[0:2], axis=1)
    gate_r = jnp.concatenate(gate_parts[2:4], axis=1)

    o_a = o_inter + jnp.concatenate([done[("same", ci)] for ci in range(n_chunks)], axis=0)
    for cj in range(n_chunks - 1):
        o_a = o_a + jnp.concatenate(
            [jnp.zeros(((cj + 1) * GLA_CHUNK, N_HEADS * DV), F32), done[("cross", cj)]], axis=0)
    o_r = jnp.concatenate([done[("ret", h)] for h in range(N_HEADS)], axis=1)
    sr_new = [staged[("ret", h)][2] for h in range(N_HEADS)]

    y_a = _head_norm_gate(o_a, gn_ref[:, 0:512], gate_a)
    y_r = _head_norm_gate(o_r, gn_ref[:, 512:1024], gate_r)
    y = jnp.concatenate([y_a, y_r], axis=1).astype(BF16)
    return x + _rms(_mm(y, wout_ref[...]), ng_ref[3:4, :]), sgt, sr_new


def _mixer_prompt_kernel(x_ref, ng_ref, wa_ref, wr_ref, wl_ref, wa2_ref, ba_ref, gn_ref, wout_ref,
                         cos_ref, sin_ref, gq_ref, gk_ref, o_ref, sg_ref, sr_ref, sgt_s, sr_s, *, robust):
    t_idx = pl.program_id(1)

    @pl.when(t_idx == 0)
    def _():
        sgt_s[...] = jnp.zeros(sgt_s.shape, F32)
        sr_s[...] = jnp.zeros(sr_s.shape, F32)

    sub = MIX_SUB_TILE
    rows4 = N_HEADS * GLA_CHUNK
    r = _iota((sub, sub), 0)
    c = _iota((sub, sub), 1)
    consts = (
        jnp.where(c <= r, 1.0, 0.0).astype(BF16),
        jnp.where(jnp.right_shift(_iota((rows4, 256), 0), 6)
                  == jnp.right_shift(_iota((rows4, 256), 1), 6), 1.0, 0.0).astype(BF16),
        jnp.right_shift(_iota((rows4, 512), 0), 6) == jnp.right_shift(_iota((rows4, 512), 1), 7),
        jnp.bitwise_and(_iota((GLA_CHUNK, rows4), 1), GLA_CHUNK - 1) <= _iota((GLA_CHUNK, rows4), 0),
        jnp.where(jnp.right_shift(_iota((512, 256), 0), 7)
                  == jnp.right_shift(_iota((512, 256), 1), 6), 1.0, 0.0),
        c <= r,
    )
    sgt = sgt_s[...]
    sr = [sr_s[h] for h in range(N_HEADS)]
    for s in range(x_ref.shape[0] // sub):
        rs = slice(s * sub, (s + 1) * sub)
        y, sgt, sr = _mixer_rows(x_ref[rs, :], cos_ref[rs, :], sin_ref[rs, :], sgt, sr, consts, robust,
                                 ng_ref, wa_ref, wr_ref, wl_ref, wa2_ref, ba_ref, gn_ref, wout_ref,
                                 gq_ref, gk_ref)
        o_ref[rs, :] = y
    sgt_s[...] = sgt
    for h in range(N_HEADS):
        sr_s[h] = sr[h]

    @pl.when(t_idx == pl.num_programs(1) - 1)
    def _():
        sg = sgt_s[...].T
        for h in range(N_HEADS):
            sg_ref[0, h] = sg[h * DK_GLA:(h + 1) * DK_GLA, h * DV:(h + 1) * DV]
        sr_ref[0] = sr_s[...]


def _mixer_prompt(x_all, robust, batch, seq, tt, layer, norm_g, wa, wr, wl, wa2, ba, gn, wout, cos2, sin2,
                  gq, gk):
    n_t = seq // tt
    x_map = lambda b, t: (b * n_t + t, 0)
    const3 = lambda b, t: (0, 0, 0)
    out_shapes = (
        jax.ShapeDtypeStruct((batch * seq, D_MODEL), F32),
        jax.ShapeDtypeStruct((batch, N_HEADS, DK_GLA, DV), F32),
        jax.ShapeDtypeStruct((batch, N_HEADS, DV, DV), F32),
    )
    weights = (norm_g, wa, wr, wl, wa2, ba, gn, wout)
    return pl.pallas_call(
        functools.partial(_mixer_prompt_kernel, robust=robust),
        out_shape=out_shapes,
        grid=(batch, n_t),
        in_specs=[pl.BlockSpec((tt, D_MODEL), x_map)]
        + [_layer_spec(w, layer, 2) for w in weights]
        + [pl.BlockSpec((tt, DV), lambda b, t: (t, 0)),
           pl.BlockSpec((tt, DV), lambda b, t: (t, 0)),
           pl.BlockSpec(gq.shape, const3),
           pl.BlockSpec(gk.shape, const3)],
        out_specs=(
            pl.BlockSpec((tt, D_MODEL), x_map),
            pl.BlockSpec((1, N_HEADS, DK_GLA, DV), lambda b, t: (b, 0, 0, 0)),
            pl.BlockSpec((1, N_HEADS, DV, DV), lambda b, t: (b, 0, 0, 0)),
        ),
        scratch_shapes=[
            pltpu.VMEM((N_HEADS * DV, N_HEADS * DK_GLA), F32),
            pltpu.VMEM((N_HEADS, DV, DV), F32),
        ],
        compiler_params=_params(2),
        name="mixer_prompt_levels" if robust else "mixer_prompt",
    )(x_all, *weights, cos2, sin2, gq, gk)


def _proj_sample_kernel(x_ref, ng_ref, wa_ref, wr_ref, wl_ref, wa2_ref, ba_ref, pa_ref, pr_ref, lg_ref):
    xn = _rms(x_ref[...], ng_ref[2:3, :]).astype(BF16)
    pa_ref[...] = _mm(xn, wa_ref[...])
    pr_ref[...] = _mm(xn, wr_ref[...])
    alo = _mm(xn, wl_ref[...]).astype(BF16)
    lg_ref[...] = _log_sigmoid(_mm(alo, wa2_ref[...]) + ba_ref[...]) * (1.0 / GATE_TEMP)


def _proj_sample(x_all, row_block, n_rows, layer, norm_g, wa, wr, wl, wa2, ba):
    const = lambda i: (0, 0)
    weights = (norm_g, wa, wr, wl, wa2, ba)
    return pl.pallas_call(
        _proj_sample_kernel,
        out_shape=(jax.ShapeDtypeStruct((n_rows, A_COLS), F32),
                   jax.ShapeDtypeStruct((n_rows, R_COLS), F32),
                   jax.ShapeDtypeStruct((n_rows, 256), F32)),
        grid=(1,),
        in_specs=[pl.BlockSpec((n_rows, D_MODEL), lambda i: (row_block, 0))]
        + [_layer_spec(w, layer, 1) for w in weights],
        out_specs=(pl.BlockSpec((n_rows, A_COLS), const),
                   pl.BlockSpec((n_rows, R_COLS), const),
                   pl.BlockSpec((n_rows, 256), const)),
        compiler_params=_params(1),
        name="proj_sample",
    )(x_all, *weights)


def _core_sample_kernel(pa_ref, pr_ref, lg_ref, cos_ref, sin_ref, gq_ref, gk_ref, sg_in, sr_in,
                        o_ref, sg_out, sr_out, *, n_tok):
    n_seq = sg_in.shape[0]
    rows4 = N_HEADS * n_tok
    tok_shift = n_tok.bit_length() - 1
    k_head_mask = jnp.where(jnp.right_shift(_iota((rows4, 256), 0), tok_shift)
                            == jnp.right_shift(_iota((rows4, 256), 1), 6), 1.0, 0.0).astype(BF16)
    v_head_mask = (jnp.right_shift(_iota((rows4, 512), 0), tok_shift)
                   == jnp.right_shift(_iota((rows4, 512), 1), 7))
    tok_row = _iota((n_tok, 256), 0)
    causal_r = _iota((n_tok, n_tok), 1) <= _iota((n_tok, n_tok), 0)
    ones = jnp.ones((2 * n_tok, DV), BF16)
    cos2 = cos_ref[...]
    sin2 = sin_ref[...]

    def one_seq(n, pa, pr, lg):
        b_rows = [lg[0:1]]
        for t in range(1, n_tok):
            b_rows.append(b_rows[-1] + lg[t:t + 1])
        bc = jnp.concatenate(b_rows, axis=0)
        b_last = b_rows[-1]
        k = pa[:, KA:KA + 256]
        v = pa[:, VA:VA + 512].astype(BF16)
        q = pa[:, QA:QA + 256] * (DK_GLA ** -0.5)
        qt = (q * jnp.exp(bc)).astype(BF16)
        kp = (k * jnp.exp(b_last - bc)).astype(BF16)
        s_old = sg_in[n].reshape(N_HEADS * DK_GLA, DV)
        s_old_b = s_old.astype(BF16)
        s_bd = jnp.concatenate(
            [jnp.where(jnp.right_shift(_iota((N_HEADS * DK_GLA, DV), 0), 6) == h, s_old_b, 0.0)
             for h in range(N_HEADS)], axis=1)
        vstack = jnp.where(v_head_mask, jnp.concatenate([v] * N_HEADS, axis=0), 0.0)
        lhs = [jnp.where(tok_row == j, q, 0.0).astype(BF16) for j in range(n_tok)]
        rhs = [_stack_heads(jnp.where(tok_row <= j, k * jnp.exp(b_rows[j] - bc), 0.0), k_head_mask)
               for j in range(n_tok)]
        a = _mm_nt(jnp.concatenate(lhs, axis=1), jnp.concatenate(rhs, axis=1)).astype(BF16)
        o_a = _mm(qt, s_bd) + _mm(a, vstack)
        lg_hi, lg_lo = _split_bf16(lg)
        b_last_t = _mm_tn(jnp.concatenate([lg_hi, lg_lo], axis=0), ones)
        u = _mm_tn(kp, v)
        u_d = jnp.concatenate(
            [u[h * DK_GLA:(h + 1) * DK_GLA, h * DV:(h + 1) * DV] for h in range(N_HEADS)], axis=0)
        sg_out[n] = (jnp.exp(b_last_t) * s_old + u_d).reshape(N_HEADS, DK_GLA, DV)
        vr = pr[:, VR:VR + 512].astype(BF16)
        o_r_heads = []
        for h in range(N_HEADS):
            hs = slice(h * DV, (h + 1) * DV)
            qh = (_rotary(pr[:, QR + h * DV:QR + (h + 1) * DV], cos2, sin2) * gq_ref[h]).astype(BF16)
            kh = (_rotary(pr[:, KR + h * DV:KR + (h + 1) * DV], cos2, sin2) * gk_ref[h]).astype(BF16)
            vh = vr[:, hs]
            ar = jnp.where(causal_r, _mm_nt(qh, kh), 0.0).astype(BF16)
            s_r = sr_in[n, h]
            o_r_heads.append(_mm(qh, s_r.astype(BF16)) + _mm(ar, vh))
            sr_out[n, h] = math.exp(n_tok * LOG_GAMMA[h]) * (s_r + _mm_tn(kh, vh))
        return jnp.concatenate([o_a] + o_r_heads, axis=1)

    seq_per_iter = 8 // n_tok

    def body(i, carry):
        rows = pl.ds(pl.multiple_of(i * 8, 8), 8)
        pa8, pr8, lg8 = pa_ref[rows, :], pr_ref[rows, :], lg_ref[rows, :]
        outs = []
        for j in range(seq_per_iter):
            sl = slice(j * n_tok, (j + 1) * n_tok)
            outs.append(one_seq(i * seq_per_iter + j, pa8[sl], pr8[sl], lg8[sl]))
        o_ref[rows, :] = jnp.concatenate(outs, axis=0)
        return carry

    lax.fori_loop(0, n_seq // seq_per_iter, body, 0)


def _core_sample(pa, pr, lg, layer, n_tok, cos2, sin2, gq, gk, state_gla, state_ret):
    n_seq = state_gla.shape[1]
    nb = SAMPLE_SEQ_BLOCK
    rows = nb * n_tok
    const2 = lambda i: (0, 0)
    const3 = lambda i: (0, 0, 0)
    out_shapes = (
        jax.ShapeDtypeStruct((n_seq * n_tok, D_MODEL), F32),
        jax.ShapeDtypeStruct(state_gla.shape[1:], F32),
        jax.ShapeDtypeStruct(state_ret.shape[1:], F32),
    )
    return pl.pallas_call(
        functools.partial(_core_sample_kernel, n_tok=n_tok),
        out_shape=out_shapes,
        grid=(n_seq // nb,),
        in_specs=[
            pl.BlockSpec((rows, A_COLS), lambda i: (i, 0)),
            pl.BlockSpec((rows, R_COLS), lambda i: (i, 0)),
            pl.BlockSpec((rows, 256), lambda i: (i, 0)),
            pl.BlockSpec(cos2.shape, const2),
            pl.BlockSpec(sin2.shape, const2),
            pl.BlockSpec(gq.shape, const3),
            pl.BlockSpec(gk.shape, const3),
            pl.BlockSpec((None, nb, N_HEADS, DK_GLA, DV), lambda i: (layer, i, 0, 0, 0)),
            pl.BlockSpec((None, nb, N_HEADS, DV, DV), lambda i: (layer, i, 0, 0, 0)),
        ],
        out_specs=(
            pl.BlockSpec((rows, D_MODEL), lambda i: (i, 0)),
            pl.BlockSpec((nb, N_HEADS, DK_GLA, DV), lambda i: (i, 0, 0, 0)),
            pl.BlockSpec((nb, N_HEADS, DV, DV), lambda i: (i, 0, 0, 0)),
        ),
        compiler_params=_params(1),
        name="core_sample",
    )(pa, pr, lg, cos2, sin2, gq, gk, state_gla, state_ret)


def _out_sample_kernel(o_ref, ga_ref, gr_ref, gn_ref, wout_ref, ng_ref, x_ref, y_ref):
    o = o_ref[...]
    y_a = _head_norm_gate(o[:, 0:512], gn_ref[:, 0:512], ga_ref[...])
    y_r = _head_norm_gate(o[:, 512:1024], gn_ref[:, 512:1024], gr_ref[...])
    y = jnp.concatenate([y_a, y_r], axis=1).astype(BF16)
    y_ref[...] = x_ref[...] + _rms(_mm(y, wout_ref[...]), ng_ref[3:4, :])


def _out_sample(o2, pa, pr, layer, gn, wout, norm_g, x_all, row_block):
    n_rows = o2.shape[0]
    const = lambda i: (0, 0)
    return pl.pallas_call(
        _out_sample_kernel,
        out_shape=jax.ShapeDtypeStruct((n_rows, D_MODEL), F32),
        grid=(1,),
        in_specs=[
            pl.BlockSpec(o2.shape, const),
            pl.BlockSpec((n_rows, 512), lambda i: (0, GA // 512)),
            pl.BlockSpec((n_rows, 512), lambda i: (0, GR // 512)),
            _layer_spec(gn, layer, 1),
            _layer_spec(wout, layer, 1),
            _layer_spec(norm_g, layer, 1),
            pl.BlockSpec((n_rows, D_MODEL), lambda i: (row_block, 0)),
        ],
        out_specs=pl.BlockSpec((n_rows, D_MODEL), const),
        compiler_params=_params(1),
        name="out_sample",
    )(o2, pa, pr, gn, wout, norm_g, x_all)


def _rope_tables(pos):
    half = DV // 2
    inv_freq = ROPE_BASE ** (-jnp.arange(half, dtype=F32) / half)
    ang = pos[:, None] * inv_freq[None, :]
    cos, sin = jnp.cos(ang), jnp.sin(ang)
    return jnp.concatenate([cos, cos], axis=-1), jnp.concatenate([-sin, sin], axis=-1)


def _decay_tables(chunk):
    steps = (jnp.arange(chunk, dtype=F32) + 1.0)[None, :, None]
    lgam = jnp.asarray(LOG_GAMMA, F32)[:, None, None]
    gq = jnp.broadcast_to(jnp.exp(steps * lgam), (N_HEADS, chunk, DV))
    gk = jnp.broadcast_to(jnp.exp(-steps * lgam) * (DV ** -0.5), (N_HEADS, chunk, DV))
    return gq, gk


def kernel(x_prompt, x_sample, state_gla, state_ret, norm_g, w1_gu, w1_down, w2_gu, w2_down,
           w_in, w_a2, b_a, gn_gla, gn_ret, w_out):
    batch, seq, _ = x_prompt.shape
    n_seq, n_tok, _ = x_sample.shape
    n_prompt = batch * seq
    n_sample = n_seq * n_tok
    assert n_prompt % FFN_SUB_TILE == 0 and n_sample % FFN_SUB_TILE == 0
    assert seq % (MIX_SUB_TILE * MIX_SUB_TILES_PER_STEP) == 0 and n_prompt % n_sample == 0
    assert n_seq % SAMPLE_SEQ_BLOCK == 0 and n_tok in (1, 2, 4, 8)
    mix_tile = MIX_SUB_TILE * MIX_SUB_TILES_PER_STEP
    sample_block = n_prompt // n_sample

    w1_gu_b, w1_dn_b = w1_gu.astype(BF16), w1_down.astype(BF16)
    w2_gu_b, w2_dn_b = w2_gu.astype(BF16), w2_down.astype(BF16)
    wa_b = w_in[..., :A_COLS].astype(BF16)
    wr_b = w_in[..., A_COLS + GATE_RANK:].astype(BF16)
    wl_b = jnp.pad(w_in[..., A_COLS:A_COLS + GATE_RANK].astype(BF16),
                   ((0, 0), (0, 0), (0, AL_PAD - GATE_RANK)))
    wa2_b = jnp.pad(w_a2.astype(BF16), ((0, 0), (0, AL_PAD - GATE_RANK), (0, 0)))
    w_out_b = w_out.astype(BF16)
    b_a3 = b_a[:, None, :]
    gn = jnp.concatenate([gn_gla, gn_ret], axis=-1)[:, None, :]

    key_bound = (math.sqrt(D_MODEL) * jnp.max(jnp.abs(norm_g[:, 2, :]), axis=-1)
                 * jnp.max(jnp.sqrt(jnp.sum(jnp.square(w_in[:, :, KA:KA + 256]), axis=1)), axis=-1))

    cos_p, sin_p = _rope_tables(jnp.arange(seq, dtype=F32))
    cos_s, sin_s = _rope_tables(PAST_LEN + jnp.arange(n_tok, dtype=F32))
    gq_p, gk_p = _decay_tables(MIX_SUB_TILE)
    gq_s, gk_s = _decay_tables(n_tok)

    xs = (x_prompt.reshape(n_prompt, D_MODEL), x_sample.reshape(n_sample, D_MODEL))
    gla_p, ret_p, gla_s, ret_s = [], [], [], []
    for l in range(DEPTH):
        x_all, gate_range = _ffn(xs, n_prompt, l, norm_g, w1_gu_b, w1_dn_b, 0,
                                 gate_weights=(wl_b, wa2_b, b_a3))
        pa, pr, lg = _proj_sample(x_all, sample_block, n_sample, l, norm_g, wa_b, wr_b, wl_b, wa2_b, b_a3)
        o2, sg_s, sr_s = _core_sample(pa, pr, lg, l, n_tok, cos_s, sin_s, gq_s, gk_s, state_gla, state_ret)
        x_s = _out_sample(o2, pa, pr, l, gn, w_out_b, norm_g, x_all, sample_block)
        bounded = jnp.logical_and(jnp.max(gate_range) < GLA_SAFE_GATE_ABS, key_bound[l] < GLA_SAFE_KEY_ABS)
        mixer = functools.partial(_mixer_prompt, batch=batch, seq=seq, tt=mix_tile, layer=l, norm_g=norm_g,
                                  wa=wa_b, wr=wr_b, wl=wl_b, wa2=wa2_b, ba=b_a3, gn=gn, wout=w_out_b,
                                  cos2=cos_p, sin2=sin_p, gq=gq_p, gk=gk_p)
        x_p, sg_p, sr_p = lax.cond(bounded, functools.partial(mixer, robust=False),
                                   functools.partial(mixer, robust=True), x_all)
        xs = _ffn((x_p, x_s), n_prompt, l, norm_g, w2_gu_b, w2_dn_b, 4, split_out=(l == DEPTH - 1))
        gla_p.append(sg_p)
        ret_p.append(sr_p)
        gla_s.append(sg_s)
        ret_s.append(sr_s)
    yp, ys = xs

    return (yp.reshape(batch, seq, D_MODEL), ys.reshape(n_seq, n_tok, D_MODEL),
            jnp.stack(gla_p), jnp.stack(ret_p), jnp.stack(gla_s), jnp.stack(ret_s))
```

```python
import functools
import math

import jax
import jax.numpy as jnp
from jax import lax
from jax.experimental import pallas as pl
from jax.experimental.pallas import tpu as pltpu

F32 = jnp.float32
BF16 = jnp.bfloat16

D_MODEL = 1024
DEPTH = 4
PAST_LEN = 16384
N_HEADS = 4
DK_GLA = 64
DV = 128
GATE_RANK = 16
GATE_TEMP = 16.0
D_FF = 2816
ROPE_BASE = 10000.0
EPS = 1e-6
GLA_CHUNK = 64

A_COLS = 1536
R_COLS = 2048
QA, KA, VA, GA = 0, 256, 512, 1024
QR, KR, VR, GR = 0, 512, 1024, 1536
AL_PAD = 128

FFN_SUB_TILE = 512
FFN_FF_TILE = 256
MIX_SUB_TILE = 256
MIX_SUB_TILES_PER_STEP = 1
SAMPLE_SEQ_BLOCK = 16
VMEM_LIMIT_BYTES = 56 * 1024 * 1024
GLA_SAFE_GATE_ABS = 14.0
GLA_SAFE_KEY_ABS = 1e9

LOG_GAMMA = tuple(math.log1p(-(2.0 ** (-5.0 - h))) for h in range(N_HEADS))


def _mm(a, b):
    return jnp.dot(a, b, preferred_element_type=F32)


def _mm_nt(a, b):
    return lax.dot_general(a, b, (((1,), (1,)), ((), ())), preferred_element_type=F32)


def _mm_tn(a, b):
    return lax.dot_general(a, b, (((0,), (0,)), ((), ())), preferred_element_type=F32)


def _rms(x, g):
    ms = jnp.mean(x * x, axis=-1, keepdims=True)
    return x * lax.rsqrt(ms + EPS) * g


def _silu(x):
    return x * (1.0 / (1.0 + jnp.exp(-x)))


def _log_sigmoid(x):
    return jnp.minimum(x, 0.0) - jnp.log1p(jnp.exp(-jnp.abs(x)))


def _split_bf16(x):
    hi = x.astype(BF16)
    lo = (x - hi.astype(F32)).astype(BF16)
    return hi, lo


def _iota(shape, dim):
    return lax.broadcasted_iota(jnp.int32, shape, dim)


def _rotary(x, cos2, sin2):
    return x * cos2 + pltpu.roll(x, DV // 2, axis=1) * sin2


def _head_norm_gate(o, gn, gate):
    outs = []
    for h in range(N_HEADS):
        oh = o[:, h * DV:(h + 1) * DV]
        ms = jnp.mean(oh * oh, axis=-1, keepdims=True)
        outs.append(oh * lax.rsqrt(ms + EPS))
    return jnp.concatenate(outs, axis=1) * gn * _silu(gate)


def _first_of_block(x, block, row):
    bit = 1
    while bit < block:
        x = jnp.where(jnp.bitwise_and(row, bit) != 0, pltpu.roll(x, bit, axis=0), x)
        bit *= 2
    return x


def _stack_heads(x, k_head_mask_b):
    return jnp.concatenate([x.astype(BF16)] * N_HEADS, axis=0) * k_head_mask_b


def _gla_level_operands(qv, kv, bv, lgv, k_head_mask_b):
    n = qv.shape[0]
    row = _iota((n, 256), 0)
    pos16, pos4, pos1 = jnp.right_shift(row, 4), jnp.bitwise_and(jnp.right_shift(row, 2), 3), jnp.bitwise_and(row, 3)
    b_first16 = jnp.concatenate([jnp.broadcast_to(bv[16 * g:16 * g + 1, :], (16, 256))
                                 for g in range(n // 16)], axis=0)
    q1 = qv * jnp.exp(bv - b_first16)
    q2 = qv * jnp.exp(bv - _first_of_block(bv, 4, row))
    k1 = [jnp.where(row < 16 * j, kv * jnp.exp(bv[16 * j:16 * j + 1, :] - bv), 0.0) for j in range(1, 4)]
    k2 = []
    for j in range(1, 4):
        ref = jnp.concatenate([jnp.broadcast_to(bv[16 * g + 4 * j:16 * g + 4 * j + 1, :], (16, 256))
                               for g in range(n // 16)], axis=0)
        k2.append(jnp.where(jnp.bitwise_and(row, 15) < 4 * j, kv * jnp.exp(ref - bv), 0.0))
    f1 = pltpu.roll(lgv, n - 1, axis=0)
    f2 = f1 + pltpu.roll(lgv, n - 2, axis=0)
    f3 = f2 + pltpu.roll(lgv, n - 3, axis=0)
    ahead = [kv, kv * jnp.exp(f1), kv * jnp.exp(f2), kv * jnp.exp(f3)]
    k3 = []
    for j in range(4):
        k_j = jnp.zeros_like(kv)
        for p in range(j + 1):
            k_j = jnp.where(pos1 == p, ahead[j - p], k_j)
        k3.append(k_j)
    lhs = []
    for q_lvl, pos, first_slot in ((q1, pos16, 1), (q2, pos4, 1), (qv, pos1, 0)):
        q_b = q_lvl.astype(BF16)
        lhs += [q_b * jnp.where(pos == j, 1.0, 0.0).astype(BF16) for j in range(first_slot, 4)]
    rhs = [_stack_heads(k_j, k_head_mask_b) for k_j in k1 + k2 + k3]
    return jnp.concatenate(lhs, axis=1), jnp.concatenate(rhs, axis=1)


def _gla_level_scores(lhs, rhs):
    n, w = lhs.shape[0], N_HEADS * DK_GLA
    s1 = _mm_nt(lhs[:, 0:3 * w], rhs[:, 0:3 * w])
    s2 = _mm_nt(lhs[:, 3 * w:6 * w], rhs[:, 3 * w:6 * w])
    s3 = _mm_nt(lhs[:, 6 * w:10 * w], rhs[:, 6 * w:10 * w])
    t_idx = _iota((n, N_HEADS * n), 0)
    s_idx = jnp.bitwise_and(_iota((n, N_HEADS * n), 1), n - 1)
    same16 = jnp.right_shift(t_idx, 4) == jnp.right_shift(s_idx, 4)
    same4 = jnp.right_shift(t_idx, 2) == jnp.right_shift(s_idx, 2)
    return jnp.where(same4, s3, jnp.where(same16, s2, s1))


def _layer_spec(arr, layer, n_grid_axes):
    zeros = (0,) * (arr.ndim - 1)
    if n_grid_axes == 1:
        return pl.BlockSpec((None,) + arr.shape[1:], lambda i: (layer,) + zeros)
    return pl.BlockSpec((None,) + arr.shape[1:], lambda i, j: (layer,) + zeros)


def _params(n_axes):
    return pltpu.CompilerParams(dimension_semantics=("arbitrary",) * n_axes,
                                vmem_limit_bytes=VMEM_LIMIT_BYTES)


def _ffn_rows(x, ng_ref, wgu_ref, wdn_ref, pre_row):
    xn = _rms(x, ng_ref[pre_row:pre_row + 1, :]).astype(BF16)
    acc = jnp.zeros(x.shape, F32)
    for c in range(D_FF // FFN_FF_TILE):
        lo = c * FFN_FF_TILE
        gate = _mm(xn, wgu_ref[:, lo:lo + FFN_FF_TILE])
        up = _mm(xn, wgu_ref[:, D_FF + lo:D_FF + lo + FFN_FF_TILE])
        act = (_silu(gate) * up).astype(BF16)
        acc = acc + _mm(act, wdn_ref[lo:lo + FFN_FF_TILE, :])
    return x + 0.5 * _rms(acc, ng_ref[pre_row + 1:pre_row + 2, :])


def _gate_range_rows(x, ng_ref, wl_ref, wa2_ref, ba_ref):
    xn = _rms(x, ng_ref[2:3, :]).astype(BF16)
    alo = _mm(xn, wl_ref[...]).astype(BF16)
    g = jnp.max(jnp.abs(_mm(alo, wa2_ref[...]) + ba_ref[...]), axis=0, keepdims=True)
    return jnp.broadcast_to(jnp.maximum(g[:, 0:128], g[:, 128:256]), (8, 128))


def _ffn_kernel(x_ref, ng_ref, wgu_ref, wdn_ref, *refs, pre_row, gate_range):
    y = _ffn_rows(x_ref[...], ng_ref, wgu_ref, wdn_ref, pre_row)
    if gate_range:
        wl_ref, wa2_ref, ba_ref, y_ref, range_ref = refs
        range_ref[0] = _gate_range_rows(y, ng_ref, wl_ref, wa2_ref, ba_ref)
    else:
        y_ref, = refs
    y_ref[...] = y


def _ffn(x, layer, norm_g, wgu, wdn, pre_row, gate_weights=None):
    tm = FFN_SUB_TILE
    n_t = x.shape[0] // tm
    rows = pl.BlockSpec((tm, D_MODEL), lambda i: (i, 0))
    weights = (norm_g, wgu, wdn) + tuple(gate_weights or ())
    out_shape, out_specs = [jax.ShapeDtypeStruct(x.shape, F32)], [rows]
    if gate_weights:
        out_shape.append(jax.ShapeDtypeStruct((n_t, 8, 128), F32))
        out_specs.append(pl.BlockSpec((1, 8, 128), lambda i: (i, 0, 0)))
    outs = pl.pallas_call(
        functools.partial(_ffn_kernel, pre_row=pre_row, gate_range=bool(gate_weights)),
        out_shape=out_shape,
        grid=(n_t,),
        in_specs=[rows] + [_layer_spec(w, layer, 1) for w in weights],
        out_specs=out_specs,
        compiler_params=_params(1),
        name="ffn",
    )(x, *weights)
    return outs if gate_weights else outs[0]


def _mixer_rows(x, cos2, sin2, sgt, sr, consts, robust, ng_ref, wa_ref, wr_ref, wl_ref, wa2_ref,
                ba_ref, gn_ref, wout_ref, gq_ref, gk_ref):
    tri, k_head_mask, v_head_mask, causal, bd_mask, causal_r = consts
    tt = x.shape[0]
    xn = _rms(x, ng_ref[2:3, :]).astype(BF16)

    alo = _mm(xn, wl_ref[...]).astype(BF16)
    gate_pre = _mm(alo, wa2_ref[...])
    q = _mm(xn, wa_ref[:, QA:QA + 256]) * (DK_GLA ** -0.5)
    k = _mm(xn, wa_ref[:, KA:KA + 256])
    v = _mm(xn, wa_ref[:, VA:VA + 512]).astype(BF16)
    lg = _log_sigmoid(gate_pre + ba_ref[...]) * (1.0 / GATE_TEMP)
    lg_hi, lg_lo = _split_bf16(lg)
    bcum = _mm(tri, lg_hi) + _mm(tri, lg_lo)
    qr = _mm(xn, wr_ref[:, QR:QR + 512])
    kr = _mm(xn, wr_ref[:, KR:KR + 512])
    b_end = bcum[tt - 1:tt, :]

    o_inter = _mm_nt((q * jnp.exp(bcum)).astype(BF16), sgt.astype(BF16))
    k_end = (k * jnp.exp(b_end - bcum)).astype(BF16)
    sgt = jnp.exp(b_end) * sgt + bd_mask * _mm_tn(v, k_end)
    vr = _mm(xn, wr_ref[:, VR:VR + 512]).astype(BF16)

    n_chunks = tt // GLA_CHUNK
    chunk = lambda j: slice(j * GLA_CHUNK, (j + 1) * GLA_CHUNK)
    chunk_last = lambda j: bcum[(j + 1) * GLA_CHUNK - 1:(j + 1) * GLA_CHUNK, :]
    vstacks = [jnp.where(v_head_mask, jnp.concatenate([v[chunk(j)]] * N_HEADS, axis=0), 0.0)
               for j in range(n_chunks)]

    def gla_cross_scores(cj):
        later = slice((cj + 1) * GLA_CHUNK, tt)
        q_later = (q[later] * jnp.exp(bcum[later] - chunk_last(cj))).astype(BF16)
        kc = k[chunk(cj)] * jnp.exp(chunk_last(cj) - bcum[chunk(cj)])
        return _mm_nt(q_later, _stack_heads(kc, k_head_mask))

    def gla_cross_values(cj, scores):
        return _mm(scores.astype(BF16), vstacks[cj])

    def gla_same_scores(ci):
        if robust:
            return _gla_level_scores(*_gla_level_operands(q[chunk(ci)], k[chunk(ci)], bcum[chunk(ci)],
                                                          lg[chunk(ci)], k_head_mask))
        ref = chunk_last(ci - 1) if ci > 0 else jnp.zeros((1, 256), F32)
        qc = (q[chunk(ci)] * jnp.exp(bcum[chunk(ci)] - ref)).astype(BF16)
        kc = k[chunk(ci)] * jnp.exp(ref - bcum[chunk(ci)])
        return jnp.where(causal, _mm_nt(qc, _stack_heads(kc, k_head_mask)), 0.0)

    def gla_same_values(ci, scores):
        return _mm(scores.astype(BF16), vstacks[ci])

    def ret_scores(h):
        hs = slice(h * DV, (h + 1) * DV)
        qh = (_rotary(qr[:, hs], cos2, sin2) * gq_ref[h]).astype(BF16)
        kh = (_rotary(kr[:, hs], cos2, sin2) * gk_ref[h]).astype(BF16)
        vh = vr[:, hs]
        a = _mm_nt(qh, kh)
        o_state = _mm(qh, sr[h].astype(BF16))
        s_new = math.exp(tt * LOG_GAMMA[h]) * (sr[h] + _mm_tn(kh, vh))
        return a, o_state, s_new, vh

    def ret_values(h, staged):
        a, o_state, _, vh = staged
        return o_state + _mm(jnp.where(causal_r, a, 0.0).astype(BF16), vh)

    stage1 = {"ret": ret_scores, "same": gla_same_scores, "cross": gla_cross_scores}
    stage2 = {"ret": ret_values, "same": gla_same_values, "cross": gla_cross_values}
    items = []
    for i in range(max(n_chunks, N_HEADS)):
        if i < N_HEADS:
            items.append(("ret", i))
        if i < n_chunks:
            items.append(("same", i))
        if i < n_chunks - 1:
            items.append(("cross", i))
    gate_cols = [(wa_ref, GA), (wa_ref, GA + 256), (wr_ref, GR), (wr_ref, GR + 256)]
    gate_parts, staged, done = [], {}, {}
    depth = 1
    for step in range(len(items) + depth):
        if step < len(items):
            kind, i = items[step]
            staged[(kind, i)] = stage1[kind](i)
        if step % 2 == 1 and len(gate_parts) < len(gate_cols):
            w_ref, off = gate_cols[len(gate_parts)]
            gate_parts.append(_mm(xn, w_ref[:, off:off + 256]))
        if step >= depth:
            kind, i = items[step - depth]
            done[(kind, i)] = stage2[kind](i, staged[(kind, i)])
    while len(gate_parts) < len(gate_cols):
        w_ref, off = gate_cols[len(gate_parts)]
        gate_parts.append(_mm(xn, w_ref[:, off:off + 256]))
    gate_a = jnp.concatenate(gate_parts[0:2], axis=1)
    gate_r = jnp.concatenate(gate_parts[2:4], axis=1)

    o_a = o_inter + jnp.concatenate([done[("same", ci)] for ci in range(n_chunks)], axis=0)
    for cj in range(n_chunks - 1):
        o_a = o_a + jnp.concatenate(
            [jnp.zeros(((cj + 1) * GLA_CHUNK, N_HEADS * DV), F32), done[("cross", cj)]], axis=0)
    o_r = jnp.concatenate([done[("ret", h)] for h in range(N_HEADS)], axis=1)
    sr_new = [staged[("ret", h)][2] for h in range(N_HEADS)]

    y_a = _head_norm_gate(o_a, gn_ref[:, 0:512], gate_a)
    y_r = _head_norm_gate(o_r, gn_ref[:, 512:1024], gate_r)
    y = jnp.concatenate([y_a, y_r], axis=1).astype(BF16)
    return x + _rms(_mm(y, wout_ref[...]), ng_ref[3:4, :]), sgt, sr_new


def _mixer_prompt_kernel(x_ref, ng_ref, wa_ref, wr_ref, wl_ref, wa2_ref, ba_ref, gn_ref, wout_ref,
                         cos_ref, sin_ref, gq_ref, gk_ref, o_ref, sg_ref, sr_ref, sgt_s, sr_s, *, robust):
    t_idx = pl.program_id(1)

    @pl.when(t_idx == 0)
    def _():
        sgt_s[...] = jnp.zeros(sgt_s.shape, F32)
        sr_s[...] = jnp.zeros(sr_s.shape, F32)

    sub = MIX_SUB_TILE
    rows4 = N_HEADS * GLA_CHUNK
    r = _iota((sub, sub), 0)
    c = _iota((sub, sub), 1)
    consts = (
        jnp.where(c <= r, 1.0, 0.0).astype(BF16),
        jnp.where(jnp.right_shift(_iota((rows4, 256), 0), 6)
                  == jnp.right_shift(_iota((rows4, 256), 1), 6), 1.0, 0.0).astype(BF16),
        jnp.right_shift(_iota((rows4, 512), 0), 6) == jnp.right_shift(_iota((rows4, 512), 1), 7),
        jnp.bitwise_and(_iota((GLA_CHUNK, rows4), 1), GLA_CHUNK - 1) <= _iota((GLA_CHUNK, rows4), 0),
        jnp.where(jnp.right_shift(_iota((512, 256), 0), 7)
                  == jnp.right_shift(_iota((512, 256), 1), 6), 1.0, 0.0),
        c <= r,
    )
    sgt = sgt_s[...]
    sr = [sr_s[h] for h in range(N_HEADS)]
    for s in range(x_ref.shape[0] // sub):
        rs = slice(s * sub, (s + 1) * sub)
        y, sgt, sr = _mixer_rows(x_ref[rs, :], cos_ref[rs, :], sin_ref[rs, :], sgt, sr, consts, robust,
                                 ng_ref, wa_ref, wr_ref, wl_ref, wa2_ref, ba_ref, gn_ref, wout_ref,
                                 gq_ref, gk_ref)
        o_ref[rs, :] = y
    sgt_s[...] = sgt
    for h in range(N_HEADS):
        sr_s[h] = sr[h]

    @pl.when(t_idx == pl.num_programs(1) - 1)
    def _():
        sg = sgt_s[...].T
        for h in range(N_HEADS):
            sg_ref[0, h] = sg[h * DK_GLA:(h + 1) * DK_GLA, h * DV:(h + 1) * DV]
        sr_ref[0] = sr_s[...]


def _mixer_prompt(x_prompt, robust, batch, seq, tt, layer, norm_g, wa, wr, wl, wa2, ba, gn, wout, cos2, sin2,
                  gq, gk):
    n_t = seq // tt
    x_map = lambda b, t: (b * n_t + t, 0)
    const3 = lambda b, t: (0, 0, 0)
    out_shapes = (
        jax.ShapeDtypeStruct((batch * seq, D_MODEL), F32),
        jax.ShapeDtypeStruct((batch, N_HEADS, DK_GLA, DV), F32),
        jax.ShapeDtypeStruct((batch, N_HEADS, DV, DV), F32),
    )
    weights = (norm_g, wa, wr, wl, wa2, ba, gn, wout)
    return pl.pallas_call(
        functools.partial(_mixer_prompt_kernel, robust=robust),
        out_shape=out_shapes,
        grid=(batch, n_t),
        in_specs=[pl.BlockSpec((tt, D_MODEL), x_map)]
        + [_layer_spec(w, layer, 2) for w in weights]
        + [pl.BlockSpec((tt, DV), lambda b, t: (t, 0)),
           pl.BlockSpec((tt, DV), lambda b, t: (t, 0)),
           pl.BlockSpec(gq.shape, const3),
           pl.BlockSpec(gk.shape, const3)],
        out_specs=(
            pl.BlockSpec((tt, D_MODEL), x_map),
            pl.BlockSpec((1, N_HEADS, DK_GLA, DV), lambda b, t: (b, 0, 0, 0)),
            pl.BlockSpec((1, N_HEADS, DV, DV), lambda b, t: (b, 0, 0, 0)),
        ),
        scratch_shapes=[
            pltpu.VMEM((N_HEADS * DV, N_HEADS * DK_GLA), F32),
            pltpu.VMEM((N_HEADS, DV, DV), F32),
        ],
        compiler_params=_params(2),
        name="mixer_prompt_levels" if robust else "mixer_prompt",
    )(x_prompt, *weights, cos2, sin2, gq, gk)


def _proj_sample_kernel(x_ref, ng_ref, wa_ref, wr_ref, wl_ref, wa2_ref, ba_ref, pa_ref, pr_ref, lg_ref):
    xn = _rms(x_ref[...], ng_ref[2:3, :]).astype(BF16)
    pa_ref[...] = _mm(xn, wa_ref[...])
    pr_ref[...] = _mm(xn, wr_ref[...])
    alo = _mm(xn, wl_ref[...]).astype(BF16)
    lg_ref[...] = _log_sigmoid(_mm(alo, wa2_ref[...]) + ba_ref[...]) * (1.0 / GATE_TEMP)


def _proj_sample(x_sample, layer, norm_g, wa, wr, wl, wa2, ba):
    n_rows = x_sample.shape[0]
    const = lambda i: (0, 0)
    weights = (norm_g, wa, wr, wl, wa2, ba)
    return pl.pallas_call(
        _proj_sample_kernel,
        out_shape=(jax.ShapeDtypeStruct((n_rows, A_COLS), F32),
                   jax.ShapeDtypeStruct((n_rows, R_COLS), F32),
                   jax.ShapeDtypeStruct((n_rows, 256), F32)),
        grid=(1,),
        in_specs=[pl.BlockSpec((n_rows, D_MODEL), const)] + [_layer_spec(w, layer, 1) for w in weights],
        out_specs=(pl.BlockSpec((n_rows, A_COLS), const),
                   pl.BlockSpec((n_rows, R_COLS), const),
                   pl.BlockSpec((n_rows, 256), const)),
        compiler_params=_params(1),
        name="proj_sample",
    )(x_sample, *weights)


def _core_sample_kernel(pa_ref, pr_ref, lg_ref, cos_ref, sin_ref, gq_ref, gk_ref, sg_in, sr_in,
                        o_ref, sg_out, sr_out, *, n_tok):
    n_seq = sg_in.shape[0]
    rows4 = N_HEADS * n_tok
    tok_shift = n_tok.bit_length() - 1
    k_head_mask = jnp.where(jnp.right_shift(_iota((rows4, 256), 0), tok_shift)
                            == jnp.right_shift(_iota((rows4, 256), 1), 6), 1.0, 0.0).astype(BF16)
    v_head_mask = (jnp.right_shift(_iota((rows4, 512), 0), tok_shift)
                   == jnp.right_shift(_iota((rows4, 512), 1), 7))
    tok_row = _iota((n_tok, 256), 0)
    causal_r = _iota((n_tok, n_tok), 1) <= _iota((n_tok, n_tok), 0)
    ones = jnp.ones((2 * n_tok, DV), BF16)
    cos2 = cos_ref[...]
    sin2 = sin_ref[...]

    def one_seq(n, pa, pr, lg):
        b_rows = [lg[0:1]]
        for t in range(1, n_tok):
            b_rows.append(b_rows[-1] + lg[t:t + 1])
        bc = jnp.concatenate(b_rows, axis=0)
        b_last = b_rows[-1]
        k = pa[:, KA:KA + 256]
        v = pa[:, VA:VA + 512].astype(BF16)
        q = pa[:, QA:QA + 256] * (DK_GLA ** -0.5)
        qt = (q * jnp.exp(bc)).astype(BF16)
        kp = (k * jnp.exp(b_last - bc)).astype(BF16)
        s_old = sg_in[n].reshape(N_HEADS * DK_GLA, DV)
        s_old_b = s_old.astype(BF16)
        s_bd = jnp.concatenate(
            [jnp.where(jnp.right_shift(_iota((N_HEADS * DK_GLA, DV), 0), 6) == h, s_old_b, 0.0)
             for h in range(N_HEADS)], axis=1)
        vstack = jnp.where(v_head_mask, jnp.concatenate([v] * N_HEADS, axis=0), 0.0)
        lhs = [jnp.where(tok_row == j, q, 0.0).astype(BF16) for j in range(n_tok)]
        rhs = [_stack_heads(jnp.where(tok_row <= j, k * jnp.exp(b_rows[j] - bc), 0.0), k_head_mask)
               for j in range(n_tok)]
        a = _mm_nt(jnp.concatenate(lhs, axis=1), jnp.concatenate(rhs, axis=1)).astype(BF16)
        o_a = _mm(qt, s_bd) + _mm(a, vstack)
        lg_hi, lg_lo = _split_bf16(lg)
        b_last_t = _mm_tn(jnp.concatenate([lg_hi, lg_lo], axis=0), ones)
        u = _mm_tn(kp, v)
        u_d = jnp.concatenate(
            [u[h * DK_GLA:(h + 1) * DK_GLA, h * DV:(h + 1) * DV] for h in range(N_HEADS)], axis=0)
        sg_out[n] = (jnp.exp(b_last_t) * s_old + u_d).reshape(N_HEADS, DK_GLA, DV)
        vr = pr[:, VR:VR + 512].astype(BF16)
        o_r_heads = []
        for h in range(N_HEADS):
            hs = slice(h * DV, (h + 1) * DV)
            qh = (_rotary(pr[:, QR + h * DV:QR + (h + 1) * DV], cos2, sin2) * gq_ref[h]).astype(BF16)
            kh = (_rotary(pr[:, KR + h * DV:KR + (h + 1) * DV], cos2, sin2) * gk_ref[h]).astype(BF16)
            vh = vr[:, hs]
            ar = jnp.where(causal_r, _mm_nt(qh, kh), 0.0).astype(BF16)
            s_r = sr_in[n, h]
            o_r_heads.append(_mm(qh, s_r.astype(BF16)) + _mm(ar, vh))
            sr_out[n, h] = math.exp(n_tok * LOG_GAMMA[h]) * (s_r + _mm_tn(kh, vh))
        return jnp.concatenate([o_a] + o_r_heads, axis=1)

    seq_per_iter = 8 // n_tok

    def body(i, carry):
        rows = pl.ds(pl.multiple_of(i * 8, 8), 8)
        pa8, pr8, lg8 = pa_ref[rows, :], pr_ref[rows, :], lg_ref[rows, :]
        outs = []
        for j in range(seq_per_iter):
            sl = slice(j * n_tok, (j + 1) * n_tok)
            outs.append(one_seq(i * seq_per_iter + j, pa8[sl], pr8[sl], lg8[sl]))
        o_ref[rows, :] = jnp.concatenate(outs, axis=0)
        return carry

    lax.fori_loop(0, n_seq // seq_per_iter, body, 0)


def _core_sample(pa, pr, lg, layer, n_tok, cos2, sin2, gq, gk, state_gla, state_ret):
    n_seq = state_gla.shape[1]
    nb = SAMPLE_SEQ_BLOCK
    rows = nb * n_tok
    const2 = lambda i: (0, 0)
    const3 = lambda i: (0, 0, 0)
    out_shapes = (
        jax.ShapeDtypeStruct((n_seq * n_tok, D_MODEL), F32),
        jax.ShapeDtypeStruct(state_gla.shape[1:], F32),
        jax.ShapeDtypeStruct(state_ret.shape[1:], F32),
    )
    return pl.pallas_call(
        functools.partial(_core_sample_kernel, n_tok=n_tok),
        out_shape=out_shapes,
        grid=(n_seq // nb,),
        in_specs=[
            pl.BlockSpec((rows, A_COLS), lambda i: (i, 0)),
            pl.BlockSpec((rows, R_COLS), lambda i: (i, 0)),
            pl.BlockSpec((rows, 256), lambda i: (i, 0)),
            pl.BlockSpec(cos2.shape, const2),
            pl.BlockSpec(sin2.shape, const2),
            pl.BlockSpec(gq.shape, const3),
            pl.BlockSpec(gk.shape, const3),
            pl.BlockSpec((None, nb, N_HEADS, DK_GLA, DV), lambda i: (layer, i, 0, 0, 0)),
            pl.BlockSpec((None, nb, N_HEADS, DV, DV), lambda i: (layer, i, 0, 0, 0)),
        ],
        out_specs=(
            pl.BlockSpec((rows, D_MODEL), lambda i: (i, 0)),
            pl.BlockSpec((nb, N_HEADS, DK_GLA, DV), lambda i: (i, 0, 0, 0)),
            pl.BlockSpec((nb, N_HEADS, DV, DV), lambda i: (i, 0, 0, 0)),
        ),
        compiler_params=_params(1),
        name="core_sample",
    )(pa, pr, lg, cos2, sin2, gq, gk, state_gla, state_ret)


def _out_sample_kernel(o_ref, ga_ref, gr_ref, gn_ref, wout_ref, ng_ref, x_ref, y_ref):
    o = o_ref[...]
    y_a = _head_norm_gate(o[:, 0:512], gn_ref[:, 0:512], ga_ref[...])
    y_r = _head_norm_gate(o[:, 512:1024], gn_ref[:, 512:1024], gr_ref[...])
    y = jnp.concatenate([y_a, y_r], axis=1).astype(BF16)
    y_ref[...] = x_ref[...] + _rms(_mm(y, wout_ref[...]), ng_ref[3:4, :])


def _out_sample(o2, pa, pr, layer, gn, wout, norm_g, x_sample):
    n_rows = o2.shape[0]
    const = lambda i: (0, 0)
    return pl.pallas_call(
        _out_sample_kernel,
        out_shape=jax.ShapeDtypeStruct((n_rows, D_MODEL), F32),
        grid=(1,),
        in_specs=[
            pl.BlockSpec(o2.shape, const),
            pl.BlockSpec((n_rows, 512), lambda i: (0, GA // 512)),
            pl.BlockSpec((n_rows, 512), lambda i: (0, GR // 512)),
            _layer_spec(gn, layer, 1),
            _layer_spec(wout, layer, 1),
            _layer_spec(norm_g, layer, 1),
            pl.BlockSpec((n_rows, D_MODEL), const),
        ],
        out_specs=pl.BlockSpec((n_rows, D_MODEL), const),
        compiler_params=_params(1),
        name="out_sample",
    )(o2, pa, pr, gn, wout, norm_g, x_sample)


def _rope_tables(pos):
    half = DV // 2
    inv_freq = ROPE_BASE ** (-jnp.arange(half, dtype=F32) / half)
    ang = pos[:, None] * inv_freq[None, :]
    cos, sin = jnp.cos(ang), jnp.sin(ang)
    return jnp.concatenate([cos, cos], axis=-1), jnp.concatenate([-sin, sin], axis=-1)


def _decay_tables(chunk):
    steps = (jnp.arange(chunk, dtype=F32) + 1.0)[None, :, None]
    lgam = jnp.asarray(LOG_GAMMA, F32)[:, None, None]
    gq = jnp.broadcast_to(jnp.exp(steps * lgam), (N_HEADS, chunk, DV))
    gk = jnp.broadcast_to(jnp.exp(-steps * lgam) * (DV ** -0.5), (N_HEADS, chunk, DV))
    return gq, gk


def kernel(x_prompt, x_sample, state_gla, state_ret, norm_g, w1_gu, w1_down, w2_gu, w2_down,
           w_in, w_a2, b_a, gn_gla, gn_ret, w_out):
    batch, seq, _ = x_prompt.shape
    n_seq, n_tok, _ = x_sample.shape
    n_prompt = batch * seq
    n_sample = n_seq * n_tok
    assert n_prompt % FFN_SUB_TILE == 0 and n_sample % FFN_SUB_TILE == 0
    assert seq % (MIX_SUB_TILE * MIX_SUB_TILES_PER_STEP) == 0
    assert n_seq % SAMPLE_SEQ_BLOCK == 0 and n_tok in (1, 2, 4, 8)
    mix_tile = MIX_SUB_TILE * MIX_SUB_TILES_PER_STEP

    w1_gu_b, w1_dn_b = w1_gu.astype(BF16), w1_down.astype(BF16)
    w2_gu_b, w2_dn_b = w2_gu.astype(BF16), w2_down.astype(BF16)
    w_in_b = w_in.astype(BF16)
    wa_b = w_in_b[..., :A_COLS]
    wr_b = w_in_b[..., A_COLS + GATE_RANK:]
    wl_b = jnp.pad(w_in_b[..., A_COLS:A_COLS + GATE_RANK], ((0, 0), (0, 0), (0, AL_PAD - GATE_RANK)))
    wa2_b = jnp.pad(w_a2.astype(BF16), ((0, 0), (0, AL_PAD - GATE_RANK), (0, 0)))
    w_out_b = w_out.astype(BF16)
    b_a3 = b_a[:, None, :]
    gn = jnp.concatenate([gn_gla, gn_ret], axis=-1)[:, None, :]

    key_bound = (math.sqrt(D_MODEL) * jnp.max(jnp.abs(norm_g[:, 2, :]), axis=-1)
                 * jnp.max(jnp.sqrt(jnp.sum(jnp.square(w_in[:, :, KA:KA + 256]), axis=1)), axis=-1))

    cos_p, sin_p = _rope_tables(jnp.arange(seq, dtype=F32))
    cos_s, sin_s = _rope_tables(PAST_LEN + jnp.arange(n_tok, dtype=F32))
    gq_p, gk_p = _decay_tables(MIX_SUB_TILE)
    gq_s, gk_s = _decay_tables(n_tok)

    xp = x_prompt.reshape(n_prompt, D_MODEL)
    xs = x_sample.reshape(n_sample, D_MODEL)
    gla_p, ret_p, gla_s, ret_s = [], [], [], []
    for l in range(DEPTH):
        xp, gate_range = _ffn(xp, l, norm_g, w1_gu_b, w1_dn_b, 0, gate_weights=(wl_b, wa2_b, b_a3))
        xs = _ffn(xs, l, norm_g, w1_gu_b, w1_dn_b, 0)
        pa, pr, lg = _proj_sample(xs, l, norm_g, wa_b, wr_b, wl_b, wa2_b, b_a3)
        o2, sg_s, sr_s = _core_sample(pa, pr, lg, l, n_tok, cos_s, sin_s, gq_s, gk_s, state_gla, state_ret)
        xs = _out_sample(o2, pa, pr, l, gn, w_out_b, norm_g, xs)
        bounded = jnp.logical_and(jnp.max(gate_range) < GLA_SAFE_GATE_ABS, key_bound[l] < GLA_SAFE_KEY_ABS)
        mixer = functools.partial(_mixer_prompt, batch=batch, seq=seq, tt=mix_tile, layer=l, norm_g=norm_g,
                                  wa=wa_b, wr=wr_b, wl=wl_b, wa2=wa2_b, ba=b_a3, gn=gn, wout=w_out_b,
                                  cos2=cos_p, sin2=sin_p, gq=gq_p, gk=gk_p)
        xp, sg_p, sr_p = lax.cond(bounded, functools.partial(mixer, robust=False),
                                  functools.partial(mixer, robust=True), xp)
        xp = _ffn(xp, l, norm_g, w2_gu_b, w2_dn_b, 4)
        xs = _ffn(xs, l, norm_g, w2_gu_b, w2_dn_b, 4)
        gla_p.append(sg_p)
        ret_p.append(sr_p)
        gla_s.append(sg_s)
        ret_s.append(sr_s)
    yp, ys = xp, xs

    return (yp.reshape(batch, seq, D_MODEL), ys.reshape(n_seq, n_tok, D_MODEL),
            jnp.stack(gla_p), jnp.stack(ret_p), jnp.stack(gla_s), jnp.stack(ret_s))
```

```python
import functools
import math

import jax
import jax.numpy as jnp
from jax import lax
from jax.experimental import pallas as pl
from jax.experimental.pallas import tpu as pltpu

F32 = jnp.float32
BF16 = jnp.bfloat16

D_MODEL = 1024
DEPTH = 4
PAST_LEN = 16384
N_HEADS = 4
DK_GLA = 64
DV = 128
GATE_RANK = 16
GATE_TEMP = 16.0
D_FF = 2816
ROPE_BASE = 10000.0
EPS = 1e-6
GLA_CHUNK = 64

A_COLS = 1536
R_COLS = 2048
QA, KA, VA, GA = 0, 256, 512, 1024
QR, KR, VR, GR = 0, 512, 1024, 1536
AL_PAD = 128

FFN_SUB_TILE = 512
FFN_FF_TILE = 256
MIX_SUB_TILE = 256
MIX_SUB_TILES_PER_STEP = 1
SAMPLE_SEQ_BLOCK = 16
VMEM_LIMIT_BYTES = 56 * 1024 * 1024
GLA_SAFE_GATE_ABS = 14.0
GLA_SAFE_KEY_ABS = 1e9

LOG_GAMMA = tuple(math.log1p(-(2.0 ** (-5.0 - h))) for h in range(N_HEADS))


def _mm(a, b):
    return jnp.dot(a, b, preferred_element_type=F32)


def _mm_nt(a, b):
    return lax.dot_general(a, b, (((1,), (1,)), ((), ())), preferred_element_type=F32)


def _mm_tn(a, b):
    return lax.dot_general(a, b, (((0,), (0,)), ((), ())), preferred_element_type=F32)


def _rms(x, g):
    ms = jnp.mean(x * x, axis=-1, keepdims=True)
    return x * lax.rsqrt(ms + EPS) * g


def _silu(x):
    return x * (1.0 / (1.0 + jnp.exp(-x)))


def _log_sigmoid(x):
    return jnp.minimum(x, 0.0) - jnp.log1p(jnp.exp(-jnp.abs(x)))


def _split_bf16(x):
    hi = x.astype(BF16)
    lo = (x - hi.astype(F32)).astype(BF16)
    return hi, lo


def _iota(shape, dim):
    return lax.broadcasted_iota(jnp.int32, shape, dim)


def _rotary(x, cos2, sin2):
    return x * cos2 + pltpu.roll(x, DV // 2, axis=1) * sin2


def _head_norm_gate(o, gn, gate):
    outs = []
    for h in range(N_HEADS):
        oh = o[:, h * DV:(h + 1) * DV]
        ms = jnp.mean(oh * oh, axis=-1, keepdims=True)
        outs.append(oh * lax.rsqrt(ms + EPS))
    return jnp.concatenate(outs, axis=1) * gn * _silu(gate)


def _first_of_block(x, block, row):
    bit = 1
    while bit < block:
        x = jnp.where(jnp.bitwise_and(row, bit) != 0, pltpu.roll(x, bit, axis=0), x)
        bit *= 2
    return x


def _stack_heads(x, k_head_mask_b):
    return jnp.concatenate([x.astype(BF16)] * N_HEADS, axis=0) * k_head_mask_b


def _gla_level_operands(qv, kv, bv, lgv, k_head_mask_b):
    n = qv.shape[0]
    row = _iota((n, 256), 0)
    pos16, pos4, pos1 = jnp.right_shift(row, 4), jnp.bitwise_and(jnp.right_shift(row, 2), 3), jnp.bitwise_and(row, 3)
    b_first16 = jnp.concatenate([jnp.broadcast_to(bv[16 * g:16 * g + 1, :], (16, 256))
                                 for g in range(n // 16)], axis=0)
    q1 = qv * jnp.exp(bv - b_first16)
    q2 = qv * jnp.exp(bv - _first_of_block(bv, 4, row))
    k1 = [jnp.where(row < 16 * j, kv * jnp.exp(bv[16 * j:16 * j + 1, :] - bv), 0.0) for j in range(1, 4)]
    k2 = []
    for j in range(1, 4):
        ref = jnp.concatenate([jnp.broadcast_to(bv[16 * g + 4 * j:16 * g + 4 * j + 1, :], (16, 256))
                               for g in range(n // 16)], axis=0)
        k2.append(jnp.where(jnp.bitwise_and(row, 15) < 4 * j, kv * jnp.exp(ref - bv), 0.0))
    f1 = pltpu.roll(lgv, n - 1, axis=0)
    f2 = f1 + pltpu.roll(lgv, n - 2, axis=0)
    f3 = f2 + pltpu.roll(lgv, n - 3, axis=0)
    ahead = [kv, kv * jnp.exp(f1), kv * jnp.exp(f2), kv * jnp.exp(f3)]
    k3 = []
    for j in range(4):
        k_j = jnp.zeros_like(kv)
        for p in range(j + 1):
            k_j = jnp.where(pos1 == p, ahead[j - p], k_j)
        k3.append(k_j)
    lhs = []
    for q_lvl, pos, first_slot in ((q1, pos16, 1), (q2, pos4, 1), (qv, pos1, 0)):
        q_b = q_lvl.astype(BF16)
        lhs += [q_b * jnp.where(pos == j, 1.0, 0.0).astype(BF16) for j in range(first_slot, 4)]
    rhs = [_stack_heads(k_j, k_head_mask_b) for k_j in k1 + k2 + k3]
    return jnp.concatenate(lhs, axis=1), jnp.concatenate(rhs, axis=1)


def _gla_level_scores(lhs, rhs):
    n, w = lhs.shape[0], N_HEADS * DK_GLA
    s1 = _mm_nt(lhs[:, 0:3 * w], rhs[:, 0:3 * w])
    s2 = _mm_nt(lhs[:, 3 * w:6 * w], rhs[:, 3 * w:6 * w])
    s3 = _mm_nt(lhs[:, 6 * w:10 * w], rhs[:, 6 * w:10 * w])
    t_idx = _iota((n, N_HEADS * n), 0)
    s_idx = jnp.bitwise_and(_iota((n, N_HEADS * n), 1), n - 1)
    same16 = jnp.right_shift(t_idx, 4) == jnp.right_shift(s_idx, 4)
    same4 = jnp.right_shift(t_idx, 2) == jnp.right_shift(s_idx, 2)
    return jnp.where(same4, s3, jnp.where(same16, s2, s1))


def _layer_spec(arr, layer, n_grid_axes):
    zeros = (0,) * (arr.ndim - 1)
    if n_grid_axes == 1:
        return pl.BlockSpec((None,) + arr.shape[1:], lambda i: (layer,) + zeros)
    return pl.BlockSpec((None,) + arr.shape[1:], lambda i, j: (layer,) + zeros)


def _params(n_axes):
    return pltpu.CompilerParams(dimension_semantics=("arbitrary",) * n_axes,
                                vmem_limit_bytes=VMEM_LIMIT_BYTES)


def _ffn_rows(x, ng_ref, wgu_ref, wdn_ref, pre_row):
    xn = _rms(x, ng_ref[pre_row:pre_row + 1, :]).astype(BF16)
    acc = jnp.zeros(x.shape, F32)
    for c in range(D_FF // FFN_FF_TILE):
        lo = c * FFN_FF_TILE
        gate = _mm(xn, wgu_ref[:, lo:lo + FFN_FF_TILE])
        up = _mm(xn, wgu_ref[:, D_FF + lo:D_FF + lo + FFN_FF_TILE])
        act = (_silu(gate) * up).astype(BF16)
        acc = acc + _mm(act, wdn_ref[lo:lo + FFN_FF_TILE, :])
    return x + 0.5 * _rms(acc, ng_ref[pre_row + 1:pre_row + 2, :])


def _ffn_kernel(x_ref, ng_ref, wgu_ref, wdn_ref, y_ref, *, pre_row):
    y_ref[...] = _ffn_rows(x_ref[...], ng_ref, wgu_ref, wdn_ref, pre_row)


def _ffn(x, layer, norm_g, wgu, wdn, pre_row):
    tm = FFN_SUB_TILE
    rows = pl.BlockSpec((tm, D_MODEL), lambda i: (i, 0))
    weights = (norm_g, wgu, wdn)
    return pl.pallas_call(
        functools.partial(_ffn_kernel, pre_row=pre_row),
        out_shape=jax.ShapeDtypeStruct(x.shape, F32),
        grid=(x.shape[0] // tm,),
        in_specs=[rows] + [_layer_spec(w, layer, 1) for w in weights],
        out_specs=rows,
        compiler_params=_params(1),
        name="ffn",
    )(x, *weights)


def _mixer_rows(x, cos2, sin2, sgt, sr, consts, robust, ng_ref, wa_ref, wr_ref, wl_ref, wa2_ref,
                ba_ref, gn_ref, wout_ref, gq_ref, gk_ref):
    tri, k_head_mask, v_head_mask, causal, bd_mask, causal_r = consts
    tt = x.shape[0]
    xn = _rms(x, ng_ref[2:3, :]).astype(BF16)

    alo = _mm(xn, wl_ref[...]).astype(BF16)
    gate_pre = _mm(alo, wa2_ref[...])
    q = _mm(xn, wa_ref[:, QA:QA + 256]) * (DK_GLA ** -0.5)
    k = _mm(xn, wa_ref[:, KA:KA + 256])
    v = _mm(xn, wa_ref[:, VA:VA + 512]).astype(BF16)
    gate_pre = gate_pre + ba_ref[...]
    gate_abs = jnp.max(jnp.abs(gate_pre), axis=0, keepdims=True)
    gate_range = jnp.broadcast_to(jnp.maximum(gate_abs[:, 0:128], gate_abs[:, 128:256]), (8, 128))
    lg = _log_sigmoid(gate_pre) * (1.0 / GATE_TEMP)
    lg_hi, lg_lo = _split_bf16(lg)
    bcum = _mm(tri, lg_hi) + _mm(tri, lg_lo)
    qr = _mm(xn, wr_ref[:, QR:QR + 512])
    kr = _mm(xn, wr_ref[:, KR:KR + 512])
    b_end = bcum[tt - 1:tt, :]

    o_inter = _mm_nt((q * jnp.exp(bcum)).astype(BF16), sgt.astype(BF16))
    k_end = (k * jnp.exp(b_end - bcum)).astype(BF16)
    sgt = jnp.exp(b_end) * sgt + bd_mask * _mm_tn(v, k_end)
    vr = _mm(xn, wr_ref[:, VR:VR + 512]).astype(BF16)

    n_chunks = tt // GLA_CHUNK
    chunk = lambda j: slice(j * GLA_CHUNK, (j + 1) * GLA_CHUNK)
    chunk_last = lambda j: bcum[(j + 1) * GLA_CHUNK - 1:(j + 1) * GLA_CHUNK, :]
    vstacks = [jnp.where(v_head_mask, jnp.concatenate([v[chunk(j)]] * N_HEADS, axis=0), 0.0)
               for j in range(n_chunks)]

    def gla_cross_scores(cj):
        later = slice((cj + 1) * GLA_CHUNK, tt)
        q_later = (q[later] * jnp.exp(bcum[later] - chunk_last(cj))).astype(BF16)
        kc = k[chunk(cj)] * jnp.exp(chunk_last(cj) - bcum[chunk(cj)])
        return _mm_nt(q_later, _stack_heads(kc, k_head_mask))

    def gla_cross_values(cj, scores):
        return _mm(scores.astype(BF16), vstacks[cj])

    def gla_same_scores(ci):
        if robust:
            return _gla_level_scores(*_gla_level_operands(q[chunk(ci)], k[chunk(ci)], bcum[chunk(ci)],
                                                          lg[chunk(ci)], k_head_mask))
        ref = chunk_last(ci - 1) if ci > 0 else jnp.zeros((1, 256), F32)
        qc = (q[chunk(ci)] * jnp.exp(bcum[chunk(ci)] - ref)).astype(BF16)
        kc = k[chunk(ci)] * jnp.exp(ref - bcum[chunk(ci)])
        return jnp.where(causal, _mm_nt(qc, _stack_heads(kc, k_head_mask)), 0.0)

    def gla_same_values(ci, scores):
        return _mm(scores.astype(BF16), vstacks[ci])

    def ret_scores(h):
        hs = slice(h * DV, (h + 1) * DV)
        qh = (_rotary(qr[:, hs], cos2, sin2) * gq_ref[h]).astype(BF16)
        kh = (_rotary(kr[:, hs], cos2, sin2) * gk_ref[h]).astype(BF16)
        vh = vr[:, hs]
        a = _mm_nt(qh, kh)
        o_state = _mm(qh, sr[h].astype(BF16))
        s_new = math.exp(tt * LOG_GAMMA[h]) * (sr[h] + _mm_tn(kh, vh))
        return a, o_state, s_new, vh

    def ret_values(h, staged):
        a, o_state, _, vh = staged
        return o_state + _mm(jnp.where(causal_r, a, 0.0).astype(BF16), vh)

    stage1 = {"ret": ret_scores, "same": gla_same_scores, "cross": gla_cross_scores}
    stage2 = {"ret": ret_values, "same": gla_same_values, "cross": gla_cross_values}
    items = []
    for i in range(max(n_chunks, N_HEADS)):
        if i < N_HEADS:
            items.append(("ret", i))
        if i < n_chunks:
            items.append(("same", i))
        if i < n_chunks - 1:
            items.append(("cross", i))
    gate_cols = [(wa_ref, GA), (wa_ref, GA + 256), (wr_ref, GR), (wr_ref, GR + 256)]
    gate_parts, staged, done = [], {}, {}
    depth = 1
    for step in range(len(items) + depth):
        if step < len(items):
            kind, i = items[step]
            staged[(kind, i)] = stage1[kind](i)
        if step % 2 == 1 and len(gate_parts) < len(gate_cols):
            w_ref, off = gate_cols[len(gate_parts)]
            gate_parts.append(_mm(xn, w_ref[:, off:off + 256]))
        if step >= depth:
            kind, i = items[step - depth]
            done[(kind, i)] = stage2[kind](i, staged[(kind, i)])
    while len(gate_parts) < len(gate_cols):
        w_ref, off = gate_cols[len(gate_parts)]
        gate_parts.append(_mm(xn, w_ref[:, off:off + 256]))
    gate_a = jnp.concatenate(gate_parts[0:2], axis=1)
    gate_r = jnp.concatenate(gate_parts[2:4], axis=1)

    o_a = o_inter + jnp.concatenate([done[("same", ci)] for ci in range(n_chunks)], axis=0)
    for cj in range(n_chunks - 1):
        o_a = o_a + jnp.concatenate(
            [jnp.zeros(((cj + 1) * GLA_CHUNK, N_HEADS * DV), F32), done[("cross", cj)]], axis=0)
    o_r = jnp.concatenate([done[("ret", h)] for h in range(N_HEADS)], axis=1)
    sr_new = [staged[("ret", h)][2] for h in range(N_HEADS)]

    y_a = _head_norm_gate(o_a, gn_ref[:, 0:512], gate_a)
    y_r = _head_norm_gate(o_r, gn_ref[:, 512:1024], gate_r)
    y = jnp.concatenate([y_a, y_r], axis=1).astype(BF16)
    return x + _rms(_mm(y, wout_ref[...]), ng_ref[3:4, :]), sgt, sr_new, gate_range


def _mixer_prompt_kernel(x_ref, ng_ref, wa_ref, wr_ref, wl_ref, wa2_ref, ba_ref, gn_ref, wout_ref,
                         cos_ref, sin_ref, gq_ref, gk_ref, o_ref, sg_ref, sr_ref, range_ref, sgt_s, sr_s, *, robust):
    t_idx = pl.program_id(1)

    @pl.when(t_idx == 0)
    def _():
        sgt_s[...] = jnp.zeros(sgt_s.shape, F32)
        sr_s[...] = jnp.zeros(sr_s.shape, F32)

    sub = MIX_SUB_TILE
    rows4 = N_HEADS * GLA_CHUNK
    r = _iota((sub, sub), 0)
    c = _iota((sub, sub), 1)
    consts = (
        jnp.where(c <= r, 1.0, 0.0).astype(BF16),
        jnp.where(jnp.right_shift(_iota((rows4, 256), 0), 6)
                  == jnp.right_shift(_iota((rows4, 256), 1), 6), 1.0, 0.0).astype(BF16),
        jnp.right_shift(_iota((rows4, 512), 0), 6) == jnp.right_shift(_iota((rows4, 512), 1), 7),
        jnp.bitwise_and(_iota((GLA_CHUNK, rows4), 1), GLA_CHUNK - 1) <= _iota((GLA_CHUNK, rows4), 0),
        jnp.where(jnp.right_shift(_iota((512, 256), 0), 7)
                  == jnp.right_shift(_iota((512, 256), 1), 6), 1.0, 0.0),
        c <= r,
    )
    sgt = sgt_s[...]
    sr = [sr_s[h] for h in range(N_HEADS)]
    for s in range(x_ref.shape[0] // sub):
        rs = slice(s * sub, (s + 1) * sub)
        y, sgt, sr, gate_range = _mixer_rows(x_ref[rs, :], cos_ref[rs, :], sin_ref[rs, :], sgt, sr, consts,
                                             robust, ng_ref, wa_ref, wr_ref, wl_ref, wa2_ref, ba_ref, gn_ref,
                                             wout_ref, gq_ref, gk_ref)
        o_ref[rs, :] = y
        first = jnp.logical_and(pl.program_id(0) == 0, jnp.logical_and(t_idx == 0, s == 0))
        range_ref[...] = jnp.where(first, gate_range, jnp.maximum(range_ref[...], gate_range))
    sgt_s[...] = sgt
    for h in range(N_HEADS):
        sr_s[h] = sr[h]

    @pl.when(t_idx == pl.num_programs(1) - 1)
    def _():
        sg = sgt_s[...].T
        for h in range(N_HEADS):
            sg_ref[0, h] = sg[h * DK_GLA:(h + 1) * DK_GLA, h * DV:(h + 1) * DV]
        sr_ref[0] = sr_s[...]


def _mixer_prompt(x_prompt, robust, batch, seq, tt, layer, norm_g, wa, wr, wl, wa2, ba, gn, wout, cos2, sin2,
                  gq, gk):
    n_t = seq // tt
    x_map = lambda b, t: (b * n_t + t, 0)
    const3 = lambda b, t: (0, 0, 0)
    out_shapes = (
        jax.ShapeDtypeStruct((batch * seq, D_MODEL), F32),
        jax.ShapeDtypeStruct((batch, N_HEADS, DK_GLA, DV), F32),
        jax.ShapeDtypeStruct((batch, N_HEADS, DV, DV), F32),
        jax.ShapeDtypeStruct((8, 128), F32),
    )
    weights = (norm_g, wa, wr, wl, wa2, ba, gn, wout)
    return pl.pallas_call(
        functools.partial(_mixer_prompt_kernel, robust=robust),
        out_shape=out_shapes,
        grid=(batch, n_t),
        in_specs=[pl.BlockSpec((tt, D_MODEL), x_map)]
        + [_layer_spec(w, layer, 2) for w in weights]
        + [pl.BlockSpec((tt, DV), lambda b, t: (t, 0)),
           pl.BlockSpec((tt, DV), lambda b, t: (t, 0)),
           pl.BlockSpec(gq.shape, const3),
           pl.BlockSpec(gk.shape, const3)],
        out_specs=(
            pl.BlockSpec((tt, D_MODEL), x_map),
            pl.BlockSpec((1, N_HEADS, DK_GLA, DV), lambda b, t: (b, 0, 0, 0)),
            pl.BlockSpec((1, N_HEADS, DV, DV), lambda b, t: (b, 0, 0, 0)),
            pl.BlockSpec((8, 128), lambda b, t: (0, 0)),
        ),
        scratch_shapes=[
            pltpu.VMEM((N_HEADS * DV, N_HEADS * DK_GLA), F32),
            pltpu.VMEM((N_HEADS, DV, DV), F32),
        ],
        compiler_params=_params(2),
        name="mixer_prompt_levels" if robust else "mixer_prompt",
    )(x_prompt, *weights, cos2, sin2, gq, gk)


def _proj_sample_kernel(x_ref, ng_ref, wa_ref, wr_ref, wl_ref, wa2_ref, ba_ref, pa_ref, pr_ref, lg_ref):
    xn = _rms(x_ref[...], ng_ref[2:3, :]).astype(BF16)
    pa_ref[...] = _mm(xn, wa_ref[...])
    pr_ref[...] = _mm(xn, wr_ref[...])
    alo = _mm(xn, wl_ref[...]).astype(BF16)
    lg_ref[...] = _log_sigmoid(_mm(alo, wa2_ref[...]) + ba_ref[...]) * (1.0 / GATE_TEMP)


def _proj_sample(x_sample, layer, norm_g, wa, wr, wl, wa2, ba):
    n_rows = x_sample.shape[0]
    const = lambda i: (0, 0)
    weights = (norm_g, wa, wr, wl, wa2, ba)
    return pl.pallas_call(
        _proj_sample_kernel,
        out_shape=(jax.ShapeDtypeStruct((n_rows, A_COLS), F32),
                   jax.ShapeDtypeStruct((n_rows, R_COLS), F32),
                   jax.ShapeDtypeStruct((n_rows, 256), F32)),
        grid=(1,),
        in_specs=[pl.BlockSpec((n_rows, D_MODEL), const)] + [_layer_spec(w, layer, 1) for w in weights],
        out_specs=(pl.BlockSpec((n_rows, A_COLS), const),
                   pl.BlockSpec((n_rows, R_COLS), const),
                   pl.BlockSpec((n_rows, 256), const)),
        compiler_params=_params(1),
        name="proj_sample",
    )(x_sample, *weights)


def _core_sample_kernel(pa_ref, pr_ref, lg_ref, cos_ref, sin_ref, gq_ref, gk_ref, sg_in, sr_in,
                        o_ref, sg_out, sr_out, *, n_tok):
    n_seq = sg_in.shape[0]
    rows4 = N_HEADS * n_tok
    tok_shift = n_tok.bit_length() - 1
    k_head_mask = jnp.where(jnp.right_shift(_iota((rows4, 256), 0), tok_shift)
                            == jnp.right_shift(_iota((rows4, 256), 1), 6), 1.0, 0.0).astype(BF16)
    v_head_mask = (jnp.right_shift(_iota((rows4, 512), 0), tok_shift)
                   == jnp.right_shift(_iota((rows4, 512), 1), 7))
    tok_row = _iota((n_tok, 256), 0)
    causal_r = _iota((n_tok, n_tok), 1) <= _iota((n_tok, n_tok), 0)
    ones = jnp.ones((2 * n_tok, DV), BF16)
    cos2 = cos_ref[...]
    sin2 = sin_ref[...]

    def one_seq(n, pa, pr, lg):
        b_rows = [lg[0:1]]
        for t in range(1, n_tok):
            b_rows.append(b_rows[-1] + lg[t:t + 1])
        bc = jnp.concatenate(b_rows, axis=0)
        b_last = b_rows[-1]
        k = pa[:, KA:KA + 256]
        v = pa[:, VA:VA + 512].astype(BF16)
        q = pa[:, QA:QA + 256] * (DK_GLA ** -0.5)
        qt = (q * jnp.exp(bc)).astype(BF16)
        kp = (k * jnp.exp(b_last - bc)).astype(BF16)
        s_old = sg_in[n].reshape(N_HEADS * DK_GLA, DV)
        s_old_b = s_old.astype(BF16)
        s_bd = jnp.concatenate(
            [jnp.where(jnp.right_shift(_iota((N_HEADS * DK_GLA, DV), 0), 6) == h, s_old_b, 0.0)
             for h in range(N_HEADS)], axis=1)
        vstack = jnp.where(v_head_mask, jnp.concatenate([v] * N_HEADS, axis=0), 0.0)
        lhs = [jnp.where(tok_row == j, q, 0.0).astype(BF16) for j in range(n_tok)]
        rhs = [_stack_heads(jnp.where(tok_row <= j, k * jnp.exp(b_rows[j] - bc), 0.0), k_head_mask)
               for j in range(n_tok)]
        a = _mm_nt(jnp.concatenate(lhs, axis=1), jnp.concatenate(rhs, axis=1)).astype(BF16)
        o_a = _mm(qt, s_bd) + _mm(a, vstack)
        lg_hi, lg_lo = _split_bf16(lg)
        b_last_t = _mm_tn(jnp.concatenate([lg_hi, lg_lo], axis=0), ones)
        u = _mm_tn(kp, v)
        u_d = jnp.concatenate(
            [u[h * DK_GLA:(h + 1) * DK_GLA, h * DV:(h + 1) * DV] for h in range(N_HEADS)], axis=0)
        sg_out[n] = (jnp.exp(b_last_t) * s_old + u_d).reshape(N_HEADS, DK_GLA, DV)
        vr = pr[:, VR:VR + 512].astype(BF16)
        o_r_heads = []
        for h in range(N_HEADS):
            hs = slice(h * DV, (h + 1) * DV)
            qh = (_rotary(pr[:, QR + h * DV:QR + (h + 1) * DV], cos2, sin2) * gq_ref[h]).astype(BF16)
            kh = (_rotary(pr[:, KR + h * DV:KR + (h + 1) * DV], cos2, sin2) * gk_ref[h]).astype(BF16)
            vh = vr[:, hs]
            ar = jnp.where(causal_r, _mm_nt(qh, kh), 0.0).astype(BF16)
            s_r = sr_in[n, h]
            o_r_heads.append(_mm(qh, s_r.astype(BF16)) + _mm(ar, vh))
            sr_out[n, h] = math.exp(n_tok * LOG_GAMMA[h]) * (s_r + _mm_tn(kh, vh))
        return jnp.concatenate([o_a] + o_r_heads, axis=1)

    seq_per_iter = 8 // n_tok

    def body(i, carry):
        rows = pl.ds(pl.multiple_of(i * 8, 8), 8)
        pa8, pr8, lg8 = pa_ref[rows, :], pr_ref[rows, :], lg_ref[rows, :]
        outs = []
        for j in range(seq_per_iter):
            sl = slice(j * n_tok, (j + 1) * n_tok)
            outs.append(one_seq(i * seq_per_iter + j, pa8[sl], pr8[sl], lg8[sl]))
        o_ref[rows, :] = jnp.concatenate(outs, axis=0)
        return carry

    lax.fori_loop(0, n_seq // seq_per_iter, body, 0, unroll=2)


def _core_sample(pa, pr, lg, layer, n_tok, cos2, sin2, gq, gk, state_gla, state_ret):
    n_seq = state_gla.shape[1]
    nb = SAMPLE_SEQ_BLOCK
    rows = nb * n_tok
    const2 = lambda i: (0, 0)
    const3 = lambda i: (0, 0, 0)
    out_shapes = (
        jax.ShapeDtypeStruct((n_seq * n_tok, D_MODEL), F32),
        jax.ShapeDtypeStruct(state_gla.shape[1:], F32),
        jax.ShapeDtypeStruct(state_ret.shape[1:], F32),
    )
    return pl.pallas_call(
        functools.partial(_core_sample_kernel, n_tok=n_tok),
        out_shape=out_shapes,
        grid=(n_seq // nb,),
        in_specs=[
            pl.BlockSpec((rows, A_COLS), lambda i: (i, 0)),
            pl.BlockSpec((rows, R_COLS), lambda i: (i, 0)),
            pl.BlockSpec((rows, 256), lambda i: (i, 0)),
            pl.BlockSpec(cos2.shape, const2),
            pl.BlockSpec(sin2.shape, const2),
            pl.BlockSpec(gq.shape, const3),
            pl.BlockSpec(gk.shape, const3),
            pl.BlockSpec((None, nb, N_HEADS, DK_GLA, DV), lambda i: (layer, i, 0, 0, 0)),
            pl.BlockSpec((None, nb, N_HEADS, DV, DV), lambda i: (layer, i, 0, 0, 0)),
        ],
        out_specs=(
            pl.BlockSpec((rows, D_MODEL), lambda i: (i, 0)),
            pl.BlockSpec((nb, N_HEADS, DK_GLA, DV), lambda i: (i, 0, 0, 0)),
            pl.BlockSpec((nb, N_HEADS, DV, DV), lambda i: (i, 0, 0, 0)),
        ),
        compiler_params=_params(1),
        name="core_sample",
    )(pa, pr, lg, cos2, sin2, gq, gk, state_gla, state_ret)


def _out_sample_kernel(o_ref, ga_ref, gr_ref, gn_ref, wout_ref, ng_ref, x_ref, y_ref):
    o = o_ref[...]
    y_a = _head_norm_gate(o[:, 0:512], gn_ref[:, 0:512], ga_ref[...])
    y_r = _head_norm_gate(o[:, 512:1024], gn_ref[:, 512:1024], gr_ref[...])
    y = jnp.concatenate([y_a, y_r], axis=1).astype(BF16)
    y_ref[...] = x_ref[...] + _rms(_mm(y, wout_ref[...]), ng_ref[3:4, :])


def _out_sample(o2, pa, pr, layer, gn, wout, norm_g, x_sample):
    n_rows = o2.shape[0]
    const = lambda i: (0, 0)
    return pl.pallas_call(
        _out_sample_kernel,
        out_shape=jax.ShapeDtypeStruct((n_rows, D_MODEL), F32),
        grid=(1,),
        in_specs=[
            pl.BlockSpec(o2.shape, const),
            pl.BlockSpec((n_rows, 512), lambda i: (0, GA // 512)),
            pl.BlockSpec((n_rows, 512), lambda i: (0, GR // 512)),
            _layer_spec(gn, layer, 1),
            _layer_spec(wout, layer, 1),
            _layer_spec(norm_g, layer, 1),
            pl.BlockSpec((n_rows, D_MODEL), const),
        ],
        out_specs=pl.BlockSpec((n_rows, D_MODEL), const),
        compiler_params=_params(1),
        name="out_sample",
    )(o2, pa, pr, gn, wout, norm_g, x_sample)


def _rope_tables(pos):
    half = DV // 2
    inv_freq = ROPE_BASE ** (-jnp.arange(half, dtype=F32) / half)
    ang = pos[:, None] * inv_freq[None, :]
    cos, sin = jnp.cos(ang), jnp.sin(ang)
    return jnp.concatenate([cos, cos], axis=-1), jnp.concatenate([-sin, sin], axis=-1)


def _decay_tables(chunk):
    steps = (jnp.arange(chunk, dtype=F32) + 1.0)[None, :, None]
    lgam = jnp.asarray(LOG_GAMMA, F32)[:, None, None]
    gq = jnp.broadcast_to(jnp.exp(steps * lgam), (N_HEADS, chunk, DV))
    gk = jnp.broadcast_to(jnp.exp(-steps * lgam) * (DV ** -0.5), (N_HEADS, chunk, DV))
    return gq, gk


def kernel(x_prompt, x_sample, state_gla, state_ret, norm_g, w1_gu, w1_down, w2_gu, w2_down,
           w_in, w_a2, b_a, gn_gla, gn_ret, w_out):
    batch, seq, _ = x_prompt.shape
    n_seq, n_tok, _ = x_sample.shape
    n_prompt = batch * seq
    n_sample = n_seq * n_tok
    assert n_prompt % FFN_SUB_TILE == 0 and n_sample % FFN_SUB_TILE == 0
    assert seq % (MIX_SUB_TILE * MIX_SUB_TILES_PER_STEP) == 0
    assert n_seq % SAMPLE_SEQ_BLOCK == 0 and n_tok in (1, 2, 4, 8)
    mix_tile = MIX_SUB_TILE * MIX_SUB_TILES_PER_STEP

    w1_gu_b, w1_dn_b = w1_gu.astype(BF16), w1_down.astype(BF16)
    w2_gu_b, w2_dn_b = w2_gu.astype(BF16), w2_down.astype(BF16)
    w_in_b = w_in.astype(BF16)
    wa_b = w_in_b[..., :A_COLS]
    wr_b = w_in_b[..., A_COLS + GATE_RANK:]
    wl_b = jnp.pad(w_in_b[..., A_COLS:A_COLS + GATE_RANK], ((0, 0), (0, 0), (0, AL_PAD - GATE_RANK)))
    wa2_b = jnp.pad(w_a2.astype(BF16), ((0, 0), (0, AL_PAD - GATE_RANK), (0, 0)))
    w_out_b = w_out.astype(BF16)
    b_a3 = b_a[:, None, :]
    gn = jnp.concatenate([gn_gla, gn_ret], axis=-1)[:, None, :]

    key_bound = (math.sqrt(D_MODEL) * jnp.max(jnp.abs(norm_g[:, 2, :]), axis=-1)
                 * jnp.max(jnp.sqrt(jnp.sum(jnp.square(w_in[:, :, KA:KA + 256]), axis=1)), axis=-1))

    cos_p, sin_p = _rope_tables(jnp.arange(seq, dtype=F32))
    cos_s, sin_s = _rope_tables(PAST_LEN + jnp.arange(n_tok, dtype=F32))
    gq_p, gk_p = _decay_tables(MIX_SUB_TILE)
    gq_s, gk_s = _decay_tables(n_tok)

    xp = x_prompt.reshape(n_prompt, D_MODEL)
    xs = x_sample.reshape(n_sample, D_MODEL)
    gla_p, ret_p, gla_s, ret_s = [], [], [], []
    for l in range(DEPTH):
        xp = _ffn(xp, l, norm_g, w1_gu_b, w1_dn_b, 0)
        xs = _ffn(xs, l, norm_g, w1_gu_b, w1_dn_b, 0)
        pa, pr, lg = _proj_sample(xs, l, norm_g, wa_b, wr_b, wl_b, wa2_b, b_a3)
        o2, sg_s, sr_s = _core_sample(pa, pr, lg, l, n_tok, cos_s, sin_s, gq_s, gk_s, state_gla, state_ret)
        xs = _out_sample(o2, pa, pr, l, gn, w_out_b, norm_g, xs)
        mixer = functools.partial(_mixer_prompt, batch=batch, seq=seq, tt=mix_tile, layer=l, norm_g=norm_g,
                                  wa=wa_b, wr=wr_b, wl=wl_b, wa2=wa2_b, ba=b_a3, gn=gn, wout=w_out_b,
                                  cos2=cos_p, sin2=sin_p, gq=gq_p, gk=gk_p)
        fast = mixer(xp, robust=False)
        bounded = jnp.logical_and(jnp.max(fast[3]) < GLA_SAFE_GATE_ABS, key_bound[l] < GLA_SAFE_KEY_ABS)
        xp, sg_p, sr_p = lax.cond(bounded, lambda x, kept: kept, lambda x, kept: mixer(x, robust=True)[:3],
                                  xp, fast[:3])
        xp = _ffn(xp, l, norm_g, w2_gu_b, w2_dn_b, 4)
        xs = _ffn(xs, l, norm_g, w2_gu_b, w2_dn_b, 4)
        gla_p.append(sg_p)
        ret_p.append(sr_p)
        gla_s.append(sg_s)
        ret_s.append(sr_s)
    yp, ys = xp, xs

    return (yp.reshape(batch, seq, D_MODEL), ys.reshape(n_seq, n_tok, D_MODEL),
            jnp.stack(gla_p), jnp.stack(ret_p), jnp.stack(gla_s), jnp.stack(ret_s))
```

```python
import functools
import math

import jax
import jax.numpy as jnp
from jax import lax
from jax.experimental import pallas as pl
from jax.experimental.pallas import tpu as pltpu

F32 = jnp.float32
BF16 = jnp.bfloat16

D_MODEL = 1024
DEPTH = 4
PAST_LEN = 16384
N_HEADS = 4
DK_GLA = 64
DV = 128
GATE_RANK = 16
GATE_TEMP = 16.0
D_FF = 2816
ROPE_BASE = 10000.0
EPS = 1e-6
GLA_CHUNK = 64

A_COLS = 1536
R_COLS = 2048
QA, KA, VA, GA = 0, 256, 512, 1024
QR, KR, VR, GR = 0, 512, 1024, 1536
AL_PAD = 128

FFN_SUB_TILE = 512
FFN_FF_TILE = 256
MIX_SUB_TILE = 256
MIX_SUB_TILES_PER_STEP = 1
SAMPLE_SEQ_BLOCK = 16
VMEM_LIMIT_BYTES = 56 * 1024 * 1024
GLA_SAFE_GATE_ABS = 14.0
GLA_SAFE_KEY_ABS = 1e9

LOG_GAMMA = tuple(math.log1p(-(2.0 ** (-5.0 - h))) for h in range(N_HEADS))


def _mm(a, b):
    return jnp.dot(a, b, preferred_element_type=F32)


def _mm_nt(a, b):
    return lax.dot_general(a, b, (((1,), (1,)), ((), ())), preferred_element_type=F32)


def _mm_tn(a, b):
    return lax.dot_general(a, b, (((0,), (0,)), ((), ())), preferred_element_type=F32)


def _rms(x, g):
    ms = jnp.mean(x * x, axis=-1, keepdims=True)
    return x * lax.rsqrt(ms + EPS) * g


def _silu(x):
    return x * (1.0 / (1.0 + jnp.exp(-x)))


def _log_sigmoid(x):
    return jnp.minimum(x, 0.0) - jnp.log1p(jnp.exp(-jnp.abs(x)))


def _split_bf16(x):
    hi = x.astype(BF16)
    lo = (x - hi.astype(F32)).astype(BF16)
    return hi, lo


def _iota(shape, dim):
    return lax.broadcasted_iota(jnp.int32, shape, dim)


def _rotary(x, cos2, sin2):
    return x * cos2 + pltpu.roll(x, DV // 2, axis=1) * sin2


def _head_norm_gate(o, gn, gate):
    outs = []
    for h in range(N_HEADS):
        oh = o[:, h * DV:(h + 1) * DV]
        ms = jnp.mean(oh * oh, axis=-1, keepdims=True)
        outs.append(oh * lax.rsqrt(ms + EPS))
    return jnp.concatenate(outs, axis=1) * gn * _silu(gate)


def _first_of_block(x, block, row):
    bit = 1
    while bit < block:
        x = jnp.where(jnp.bitwise_and(row, bit) != 0, pltpu.roll(x, bit, axis=0), x)
        bit *= 2
    return x


def _stack_heads(x, k_head_mask_b):
    return jnp.concatenate([x.astype(BF16)] * N_HEADS, axis=0) * k_head_mask_b


def _gla_level_operands(qv, kv, bv, lgv, k_head_mask_b):
    n = qv.shape[0]
    row = _iota((n, 256), 0)
    pos16, pos4, pos1 = jnp.right_shift(row, 4), jnp.bitwise_and(jnp.right_shift(row, 2), 3), jnp.bitwise_and(row, 3)
    b_first16 = jnp.concatenate([jnp.broadcast_to(bv[16 * g:16 * g + 1, :], (16, 256))
                                 for g in range(n // 16)], axis=0)
    q1 = qv * jnp.exp(bv - b_first16)
    q2 = qv * jnp.exp(bv - _first_of_block(bv, 4, row))
    k1 = [jnp.where(row < 16 * j, kv * jnp.exp(bv[16 * j:16 * j + 1, :] - bv), 0.0) for j in range(1, 4)]
    k2 = []
    for j in range(1, 4):
        ref = jnp.concatenate([jnp.broadcast_to(bv[16 * g + 4 * j:16 * g + 4 * j + 1, :], (16, 256))
                               for g in range(n // 16)], axis=0)
        k2.append(jnp.where(jnp.bitwise_and(row, 15) < 4 * j, kv * jnp.exp(ref - bv), 0.0))
    f1 = pltpu.roll(lgv, n - 1, axis=0)
    f2 = f1 + pltpu.roll(lgv, n - 2, axis=0)
    f3 = f2 + pltpu.roll(lgv, n - 3, axis=0)
    ahead = [kv, kv * jnp.exp(f1), kv * jnp.exp(f2), kv * jnp.exp(f3)]
    k3 = []
    for j in range(4):
        k_j = jnp.zeros_like(kv)
        for p in range(j + 1):
            k_j = jnp.where(pos1 == p, ahead[j - p], k_j)
        k3.append(k_j)
    lhs = []
    for q_lvl, pos, first_slot in ((q1, pos16, 1), (q2, pos4, 1), (qv, pos1, 0)):
        q_b = q_lvl.astype(BF16)
        lhs += [q_b * jnp.where(pos == j, 1.0, 0.0).astype(BF16) for j in range(first_slot, 4)]
    rhs = [_stack_heads(k_j, k_head_mask_b) for k_j in k1 + k2 + k3]
    return jnp.concatenate(lhs, axis=1), jnp.concatenate(rhs, axis=1)


def _gla_level_scores(lhs, rhs):
    n, w = lhs.shape[0], N_HEADS * DK_GLA
    s1 = _mm_nt(lhs[:, 0:3 * w], rhs[:, 0:3 * w])
    s2 = _mm_nt(lhs[:, 3 * w:6 * w], rhs[:, 3 * w:6 * w])
    s3 = _mm_nt(lhs[:, 6 * w:10 * w], rhs[:, 6 * w:10 * w])
    t_idx = _iota((n, N_HEADS * n), 0)
    s_idx = jnp.bitwise_and(_iota((n, N_HEADS * n), 1), n - 1)
    same16 = jnp.right_shift(t_idx, 4) == jnp.right_shift(s_idx, 4)
    same4 = jnp.right_shift(t_idx, 2) == jnp.right_shift(s_idx, 2)
    return jnp.where(same4, s3, jnp.where(same16, s2, s1))


def _layer_spec(arr, layer, n_grid_axes):
    zeros = (0,) * (arr.ndim - 1)
    if n_grid_axes == 1:
        return pl.BlockSpec((None,) + arr.shape[1:], lambda i: (layer,) + zeros)
    return pl.BlockSpec((None,) + arr.shape[1:], lambda i, j: (layer,) + zeros)


def _params(n_axes):
    return pltpu.CompilerParams(dimension_semantics=("arbitrary",) * n_axes,
                                vmem_limit_bytes=VMEM_LIMIT_BYTES)


def _ffn_rows(x, ng_ref, wgu_ref, wdn_ref, pre_row):
    xn = _rms(x, ng_ref[pre_row:pre_row + 1, :]).astype(BF16)
    acc = jnp.zeros(x.shape, F32)
    for c in range(D_FF // FFN_FF_TILE):
        lo = c * FFN_FF_TILE
        gate = _mm(xn, wgu_ref[:, lo:lo + FFN_FF_TILE])
        up = _mm(xn, wgu_ref[:, D_FF + lo:D_FF + lo + FFN_FF_TILE])
        act = (_silu(gate) * up).astype(BF16)
        acc = acc + _mm(act, wdn_ref[lo:lo + FFN_FF_TILE, :])
    return x + 0.5 * _rms(acc, ng_ref[pre_row + 1:pre_row + 2, :])


def _ffn_kernel(xp_ref, xs_ref, ng_ref, wgu_ref, wdn_ref, yp_ref, ys_ref, *, pre_row, n_prompt_tiles):
    is_prompt = pl.program_id(0) < n_prompt_tiles
    x = jnp.where(is_prompt, xp_ref[...], xs_ref[...])
    y = _ffn_rows(x, ng_ref, wgu_ref, wdn_ref, pre_row)

    @pl.when(is_prompt)
    def _():
        yp_ref[...] = y

    @pl.when(jnp.logical_not(is_prompt))
    def _():
        ys_ref[...] = y


def _ffn(xp, xs, layer, norm_g, wgu, wdn, pre_row):
    tm = FFN_SUB_TILE
    n_p, n_s = xp.shape[0] // tm, xs.shape[0] // tm
    specs = [pl.BlockSpec((tm, D_MODEL), lambda i: (jnp.minimum(i, n_p - 1), 0)),
             pl.BlockSpec((tm, D_MODEL), lambda i: (jnp.maximum(i - n_p, 0), 0))]
    weights = (norm_g, wgu, wdn)
    return pl.pallas_call(
        functools.partial(_ffn_kernel, pre_row=pre_row, n_prompt_tiles=n_p),
        out_shape=(jax.ShapeDtypeStruct(xp.shape, F32), jax.ShapeDtypeStruct(xs.shape, F32)),
        grid=(n_p + n_s,),
        in_specs=specs + [_layer_spec(w, layer, 1) for w in weights],
        out_specs=specs,
        compiler_params=_params(1),
        name="ffn",
    )(xp, xs, *weights)


def _mixer_rows(x, cos2, sin2, sgt, sr, consts, robust, ng_ref, wa_ref, wr_ref, wl_ref, wa2_ref,
                ba_ref, gn_ref, wout_ref, gq_ref, gk_ref):
    tri, k_head_mask, v_head_mask, causal, bd_mask, causal_r = consts
    tt = x.shape[0]
    xn = _rms(x, ng_ref[2:3, :]).astype(BF16)

    alo = _mm(xn, wl_ref[...]).astype(BF16)
    gate_pre = _mm(alo, wa2_ref[...])
    q = _mm(xn, wa_ref[:, QA:QA + 256]) * (DK_GLA ** -0.5)
    k = _mm(xn, wa_ref[:, KA:KA + 256])
    v = _mm(xn, wa_ref[:, VA:VA + 512]).astype(BF16)
    gate_pre = gate_pre + ba_ref[...]
    gate_abs = jnp.max(jnp.abs(gate_pre), axis=0, keepdims=True)
    gate_range = jnp.broadcast_to(jnp.maximum(gate_abs[:, 0:128], gate_abs[:, 128:256]), (8, 128))
    lg = _log_sigmoid(gate_pre) * (1.0 / GATE_TEMP)
    lg_hi, lg_lo = _split_bf16(lg)
    bcum = _mm(tri, lg_hi) + _mm(tri, lg_lo)
    qr = _mm(xn, wr_ref[:, QR:QR + 512])
    kr = _mm(xn, wr_ref[:, KR:KR + 512])
    b_end = bcum[tt - 1:tt, :]

    o_inter = _mm_nt((q * jnp.exp(bcum)).astype(BF16), sgt.astype(BF16))
    k_end = (k * jnp.exp(b_end - bcum)).astype(BF16)
    sgt = jnp.exp(b_end) * sgt + bd_mask * _mm_tn(v, k_end)
    vr = _mm(xn, wr_ref[:, VR:VR + 512]).astype(BF16)

    n_chunks = tt // GLA_CHUNK
    chunk = lambda j: slice(j * GLA_CHUNK, (j + 1) * GLA_CHUNK)
    chunk_last = lambda j: bcum[(j + 1) * GLA_CHUNK - 1:(j + 1) * GLA_CHUNK, :]
    vstacks = [jnp.where(v_head_mask, jnp.concatenate([v[chunk(j)]] * N_HEADS, axis=0), 0.0)
               for j in range(n_chunks)]

    def gla_cross_scores(cj):
        later = slice((cj + 1) * GLA_CHUNK, tt)
        q_later = (q[later] * jnp.exp(bcum[later] - chunk_last(cj))).astype(BF16)
        kc = k[chunk(cj)] * jnp.exp(chunk_last(cj) - bcum[chunk(cj)])
        return _mm_nt(q_later, _stack_heads(kc, k_head_mask))

    def gla_cross_values(cj, scores):
        return _mm(scores.astype(BF16), vstacks[cj])

    def gla_same_scores(ci):
        if robust:
            return _gla_level_scores(*_gla_level_operands(q[chunk(ci)], k[chunk(ci)], bcum[chunk(ci)],
                                                          lg[chunk(ci)], k_head_mask))
        ref = chunk_last(ci - 1) if ci > 0 else jnp.zeros((1, 256), F32)
        qc = (q[chunk(ci)] * jnp.exp(bcum[chunk(ci)] - ref)).astype(BF16)
        kc = k[chunk(ci)] * jnp.exp(ref - bcum[chunk(ci)])
        return jnp.where(causal, _mm_nt(qc, _stack_heads(kc, k_head_mask)), 0.0)

    def gla_same_values(ci, scores):
        return _mm(scores.astype(BF16), vstacks[ci])

    def ret_scores(h):
        hs = slice(h * DV, (h + 1) * DV)
        qh = (_rotary(qr[:, hs], cos2, sin2) * gq_ref[h]).astype(BF16)
        kh = (_rotary(kr[:, hs], cos2, sin2) * gk_ref[h]).astype(BF16)
        vh = vr[:, hs]
        a = _mm_nt(qh, kh)
        o_state = _mm(qh, sr[h].astype(BF16))
        s_new = math.exp(tt * LOG_GAMMA[h]) * (sr[h] + _mm_tn(kh, vh))
        return a, o_state, s_new, vh

    def ret_values(h, staged):
        a, o_state, _, vh = staged
        return o_state + _mm(jnp.where(causal_r, a, 0.0).astype(BF16), vh)

    stage1 = {"ret": ret_scores, "same": gla_same_scores, "cross": gla_cross_scores}
    stage2 = {"ret": ret_values, "same": gla_same_values, "cross": gla_cross_values}
    items = [("cross", i) for i in range(n_chunks - 1)]
    for i in range(max(n_chunks, N_HEADS)):
        if i < n_chunks:
            items.append(("same", i))
        if i < N_HEADS:
            items.append(("ret", i))
    gate_cols = [(wa_ref, GA), (wa_ref, GA + 256), (wr_ref, GR), (wr_ref, GR + 256)]
    gate_parts, staged, done = [], {}, {}
    depth = 1
    for step in range(len(items) + depth):
        if step < len(items):
            kind, i = items[step]
            staged[(kind, i)] = stage1[kind](i)
        if step % 2 == 1 and len(gate_parts) < len(gate_cols):
            w_ref, off = gate_cols[len(gate_parts)]
            gate_parts.append(_mm(xn, w_ref[:, off:off + 256]))
        if step >= depth:
            kind, i = items[step - depth]
            done[(kind, i)] = stage2[kind](i, staged[(kind, i)])
    while len(gate_parts) < len(gate_cols):
        w_ref, off = gate_cols[len(gate_parts)]
        gate_parts.append(_mm(xn, w_ref[:, off:off + 256]))
    gate_a = jnp.concatenate(gate_parts[0:2], axis=1)
    gate_r = jnp.concatenate(gate_parts[2:4], axis=1)

    o_a = o_inter + jnp.concatenate([done[("same", ci)] for ci in range(n_chunks)], axis=0)
    for cj in range(n_chunks - 1):
        o_a = o_a + jnp.concatenate(
            [jnp.zeros(((cj + 1) * GLA_CHUNK, N_HEADS * DV), F32), done[("cross", cj)]], axis=0)
    o_r = jnp.concatenate([done[("ret", h)] for h in range(N_HEADS)], axis=1)
    sr_new = [staged[("ret", h)][2] for h in range(N_HEADS)]

    y_a = _head_norm_gate(o_a, gn_ref[:, 0:512], gate_a)
    y_r = _head_norm_gate(o_r, gn_ref[:, 512:1024], gate_r)
    y = jnp.concatenate([y_a, y_r], axis=1).astype(BF16)
    return x + _rms(_mm(y, wout_ref[...]), ng_ref[3:4, :]), sgt, sr_new, gate_range


def _mixer_prompt_kernel(x_ref, ng_ref, wa_ref, wr_ref, wl_ref, wa2_ref, ba_ref, gn_ref, wout_ref,
                         cos_ref, sin_ref, gq_ref, gk_ref, o_ref, sg_ref, sr_ref, range_ref, sgt_s, sr_s, *, robust):
    t_idx = pl.program_id(1)

    @pl.when(t_idx == 0)
    def _():
        sgt_s[...] = jnp.zeros(sgt_s.shape, F32)
        sr_s[...] = jnp.zeros(sr_s.shape, F32)

    sub = MIX_SUB_TILE
    rows4 = N_HEADS * GLA_CHUNK
    r = _iota((sub, sub), 0)
    c = _iota((sub, sub), 1)
    consts = (
        jnp.where(c <= r, 1.0, 0.0).astype(BF16),
        jnp.where(jnp.right_shift(_iota((rows4, 256), 0), 6)
                  == jnp.right_shift(_iota((rows4, 256), 1), 6), 1.0, 0.0).astype(BF16),
        jnp.right_shift(_iota((rows4, 512), 0), 6) == jnp.right_shift(_iota((rows4, 512), 1), 7),
        jnp.bitwise_and(_iota((GLA_CHUNK, rows4), 1), GLA_CHUNK - 1) <= _iota((GLA_CHUNK, rows4), 0),
        jnp.where(jnp.right_shift(_iota((512, 256), 0), 7)
                  == jnp.right_shift(_iota((512, 256), 1), 6), 1.0, 0.0),
        c <= r,
    )
    sgt = sgt_s[...]
    sr = [sr_s[h] for h in range(N_HEADS)]
    for s in range(x_ref.shape[0] // sub):
        rs = slice(s * sub, (s + 1) * sub)
        y, sgt, sr, gate_range = _mixer_rows(x_ref[rs, :], cos_ref[rs, :], sin_ref[rs, :], sgt, sr, consts,
                                             robust, ng_ref, wa_ref, wr_ref, wl_ref, wa2_ref, ba_ref, gn_ref,
                                             wout_ref, gq_ref, gk_ref)
        o_ref[rs, :] = y
        first = jnp.logical_and(pl.program_id(0) == 0, jnp.logical_and(t_idx == 0, s == 0))
        range_ref[...] = jnp.where(first, gate_range, jnp.maximum(range_ref[...], gate_range))
    sgt_s[...] = sgt
    for h in range(N_HEADS):
        sr_s[h] = sr[h]

    @pl.when(t_idx == pl.num_programs(1) - 1)
    def _():
        sg = sgt_s[...].T
        for h in range(N_HEADS):
            sg_ref[0, h] = sg[h * DK_GLA:(h + 1) * DK_GLA, h * DV:(h + 1) * DV]
        sr_ref[0] = sr_s[...]


def _mixer_prompt(x_prompt, robust, batch, seq, tt, layer, norm_g, wa, wr, wl, wa2, ba, gn, wout, cos2, sin2,
                  gq, gk):
    n_t = seq // tt
    x_map = lambda b, t: (b * n_t + t, 0)
    const3 = lambda b, t: (0, 0, 0)
    out_shapes = (
        jax.ShapeDtypeStruct((batch * seq, D_MODEL), F32),
        jax.ShapeDtypeStruct((batch, N_HEADS, DK_GLA, DV), F32),
        jax.ShapeDtypeStruct((batch, N_HEADS, DV, DV), F32),
        jax.ShapeDtypeStruct((8, 128), F32),
    )
    weights = (norm_g, wa, wr, wl, wa2, ba, gn, wout)
    return pl.pallas_call(
        functools.partial(_mixer_prompt_kernel, robust=robust),
        out_shape=out_shapes,
        grid=(batch, n_t),
        in_specs=[pl.BlockSpec((tt, D_MODEL), x_map)]
        + [_layer_spec(w, layer, 2) for w in weights]
        + [pl.BlockSpec((tt, DV), lambda b, t: (t, 0)),
           pl.BlockSpec((tt, DV), lambda b, t: (t, 0)),
           pl.BlockSpec(gq.shape, const3),
           pl.BlockSpec(gk.shape, const3)],
        out_specs=(
            pl.BlockSpec((tt, D_MODEL), x_map),
            pl.BlockSpec((1, N_HEADS, DK_GLA, DV), lambda b, t: (b, 0, 0, 0)),
            pl.BlockSpec((1, N_HEADS, DV, DV), lambda b, t: (b, 0, 0, 0)),
            pl.BlockSpec((8, 128), lambda b, t: (0, 0)),
        ),
        scratch_shapes=[
            pltpu.VMEM((N_HEADS * DV, N_HEADS * DK_GLA), F32),
            pltpu.VMEM((N_HEADS, DV, DV), F32),
        ],
        compiler_params=_params(2),
        name="mixer_prompt_levels" if robust else "mixer_prompt",
    )(x_prompt, *weights, cos2, sin2, gq, gk)


def _proj_sample_kernel(x_ref, ng_ref, wa_ref, wr_ref, wl_ref, wa2_ref, ba_ref, pa_ref, pr_ref, lg_ref):
    xn = _rms(x_ref[...], ng_ref[2:3, :]).astype(BF16)
    pa_ref[...] = _mm(xn, wa_ref[...])
    pr_ref[...] = _mm(xn, wr_ref[...])
    alo = _mm(xn, wl_ref[...]).astype(BF16)
    lg_ref[...] = _log_sigmoid(_mm(alo, wa2_ref[...]) + ba_ref[...]) * (1.0 / GATE_TEMP)


def _proj_sample(x_sample, layer, norm_g, wa, wr, wl, wa2, ba):
    n_rows = x_sample.shape[0]
    const = lambda i: (0, 0)
    weights = (norm_g, wa, wr, wl, wa2, ba)
    return pl.pallas_call(
        _proj_sample_kernel,
        out_shape=(jax.ShapeDtypeStruct((n_rows, A_COLS), F32),
                   jax.ShapeDtypeStruct((n_rows, R_COLS), F32),
                   jax.ShapeDtypeStruct((n_rows, 256), F32)),
        grid=(1,),
        in_specs=[pl.BlockSpec((n_rows, D_MODEL), const)] + [_layer_spec(w, layer, 1) for w in weights],
        out_specs=(pl.BlockSpec((n_rows, A_COLS), const),
                   pl.BlockSpec((n_rows, R_COLS), const),
                   pl.BlockSpec((n_rows, 256), const)),
        compiler_params=_params(1),
        name="proj_sample",
    )(x_sample, *weights)


def _core_sample_kernel(pa_ref, pr_ref, lg_ref, cos_ref, sin_ref, gq_ref, gk_ref, sg_in, sr_in,
                        o_ref, sg_out, sr_out, *, n_tok):
    n_seq = sg_in.shape[0]
    rows4 = N_HEADS * n_tok
    tok_shift = n_tok.bit_length() - 1
    k_head_mask = jnp.where(jnp.right_shift(_iota((rows4, 256), 0), tok_shift)
                            == jnp.right_shift(_iota((rows4, 256), 1), 6), 1.0, 0.0).astype(BF16)
    v_head_mask = (jnp.right_shift(_iota((rows4, 512), 0), tok_shift)
                   == jnp.right_shift(_iota((rows4, 512), 1), 7))
    tok_row = _iota((n_tok, 256), 0)
    causal_r = _iota((n_tok, n_tok), 1) <= _iota((n_tok, n_tok), 0)
    ones = jnp.ones((2 * n_tok, DV), BF16)
    cos2 = cos_ref[...]
    sin2 = sin_ref[...]

    def one_seq(n, pa, pr, lg):
        b_rows = [lg[0:1]]
        for t in range(1, n_tok):
            b_rows.append(b_rows[-1] + lg[t:t + 1])
        bc = jnp.concatenate(b_rows, axis=0)
        b_last = b_rows[-1]
        k = pa[:, KA:KA + 256]
        v = pa[:, VA:VA + 512].astype(BF16)
        q = pa[:, QA:QA + 256] * (DK_GLA ** -0.5)
        qt = (q * jnp.exp(bc)).astype(BF16)
        kp = (k * jnp.exp(b_last - bc)).astype(BF16)
        s_old = sg_in[n].reshape(N_HEADS * DK_GLA, DV)
        s_old_b = s_old.astype(BF16)
        s_bd = jnp.concatenate(
            [jnp.where(jnp.right_shift(_iota((N_HEADS * DK_GLA, DV), 0), 6) == h, s_old_b, 0.0)
             for h in range(N_HEADS)], axis=1)
        vstack = jnp.where(v_head_mask, jnp.concatenate([v] * N_HEADS, axis=0), 0.0)
        lhs = [jnp.where(tok_row == j, q, 0.0).astype(BF16) for j in range(n_tok)]
        rhs = [_stack_heads(jnp.where(tok_row <= j, k * jnp.exp(b_rows[j] - bc), 0.0), k_head_mask)
               for j in range(n_tok)]
        a = _mm_nt(jnp.concatenate(lhs, axis=1), jnp.concatenate(rhs, axis=1)).astype(BF16)
        o_a = _mm(qt, s_bd) + _mm(a, vstack)
        lg_hi, lg_lo = _split_bf16(lg)
        b_last_t = _mm_tn(jnp.concatenate([lg_hi, lg_lo], axis=0), ones)
        u = _mm_tn(kp, v)
        u_d = jnp.concatenate(
            [u[h * DK_GLA:(h + 1) * DK_GLA, h * DV:(h + 1) * DV] for h in range(N_HEADS)], axis=0)
        sg_out[n] = (jnp.exp(b_last_t) * s_old + u_d).reshape(N_HEADS, DK_GLA, DV)
        vr = pr[:, VR:VR + 512].astype(BF16)
        o_r_heads = []
        for h in range(N_HEADS):
            hs = slice(h * DV, (h + 1) * DV)
            qh = (_rotary(pr[:, QR + h * DV:QR + (h + 1) * DV], cos2, sin2) * gq_ref[h]).astype(BF16)
            kh = (_rotary(pr[:, KR + h * DV:KR + (h + 1) * DV], cos2, sin2) * gk_ref[h]).astype(BF16)
            vh = vr[:, hs]
            ar = jnp.where(causal_r, _mm_nt(qh, kh), 0.0).astype(BF16)
            s_r = sr_in[n, h]
            o_r_heads.append(_mm(qh, s_r.astype(BF16)) + _mm(ar, vh))
            sr_out[n, h] = math.exp(n_tok * LOG_GAMMA[h]) * (s_r + _mm_tn(kh, vh))
        return jnp.concatenate([o_a] + o_r_heads, axis=1)

    seq_per_iter = 8 // n_tok

    def body(i, carry):
        rows = pl.ds(pl.multiple_of(i * 8, 8), 8)
        pa8, pr8, lg8 = pa_ref[rows, :], pr_ref[rows, :], lg_ref[rows, :]
        outs = []
        for j in range(seq_per_iter):
            sl = slice(j * n_tok, (j + 1) * n_tok)
            outs.append(one_seq(i * seq_per_iter + j, pa8[sl], pr8[sl], lg8[sl]))
        o_ref[rows, :] = jnp.concatenate(outs, axis=0)
        return carry

    lax.fori_loop(0, n_seq // seq_per_iter, body, 0, unroll=2)


def _core_sample(pa, pr, lg, layer, n_tok, cos2, sin2, gq, gk, state_gla, state_ret):
    n_seq = state_gla.shape[1]
    nb = SAMPLE_SEQ_BLOCK
    rows = nb * n_tok
    const2 = lambda i: (0, 0)
    const3 = lambda i: (0, 0, 0)
    out_shapes = (
        jax.ShapeDtypeStruct((n_seq * n_tok, D_MODEL), F32),
        jax.ShapeDtypeStruct(state_gla.shape[1:], F32),
        jax.ShapeDtypeStruct(state_ret.shape[1:], F32),
    )
    return pl.pallas_call(
        functools.partial(_core_sample_kernel, n_tok=n_tok),
        out_shape=out_shapes,
        grid=(n_seq // nb,),
        in_specs=[
            pl.BlockSpec((rows, A_COLS), lambda i: (i, 0)),
            pl.BlockSpec((rows, R_COLS), lambda i: (i, 0)),
            pl.BlockSpec((rows, 256), lambda i: (i, 0)),
            pl.BlockSpec(cos2.shape, const2),
            pl.BlockSpec(sin2.shape, const2),
            pl.BlockSpec(gq.shape, const3),
            pl.BlockSpec(gk.shape, const3),
            pl.BlockSpec((None, nb, N_HEADS, DK_GLA, DV), lambda i: (layer, i, 0, 0, 0)),
            pl.BlockSpec((None, nb, N_HEADS, DV, DV), lambda i: (layer, i, 0, 0, 0)),
        ],
        out_specs=(
            pl.BlockSpec((rows, D_MODEL), lambda i: (i, 0)),
            pl.BlockSpec((nb, N_HEADS, DK_GLA, DV), lambda i: (i, 0, 0, 0)),
            pl.BlockSpec((nb, N_HEADS, DV, DV), lambda i: (i, 0, 0, 0)),
        ),
        compiler_params=_params(1),
        name="core_sample",
    )(pa, pr, lg, cos2, sin2, gq, gk, state_gla, state_ret)


def _out_sample_kernel(o_ref, ga_ref, gr_ref, gn_ref, wout_ref, ng_ref, x_ref, y_ref):
    o = o_ref[...]
    y_a = _head_norm_gate(o[:, 0:512], gn_ref[:, 0:512], ga_ref[...])
    y_r = _head_norm_gate(o[:, 512:1024], gn_ref[:, 512:1024], gr_ref[...])
    y = jnp.concatenate([y_a, y_r], axis=1).astype(BF16)
    y_ref[...] = x_ref[...] + _rms(_mm(y, wout_ref[...]), ng_ref[3:4, :])


def _out_sample(o2, pa, pr, layer, gn, wout, norm_g, x_sample):
    n_rows = o2.shape[0]
    const = lambda i: (0, 0)
    return pl.pallas_call(
        _out_sample_kernel,
        out_shape=jax.ShapeDtypeStruct((n_rows, D_MODEL), F32),
        grid=(1,),
        in_specs=[
            pl.BlockSpec(o2.shape, const),
            pl.BlockSpec((n_rows, 512), lambda i: (0, GA // 512)),
            pl.BlockSpec((n_rows, 512), lambda i: (0, GR // 512)),
            _layer_spec(gn, layer, 1),
            _layer_spec(wout, layer, 1),
            _layer_spec(norm_g, layer, 1),
            pl.BlockSpec((n_rows, D_MODEL), const),
        ],
        out_specs=pl.BlockSpec((n_rows, D_MODEL), const),
        compiler_params=_params(1),
        name="out_sample",
    )(o2, pa, pr, gn, wout, norm_g, x_sample)


def _rope_tables(pos):
    half = DV // 2
    inv_freq = ROPE_BASE ** (-jnp.arange(half, dtype=F32) / half)
    ang = pos[:, None] * inv_freq[None, :]
    cos, sin = jnp.cos(ang), jnp.sin(ang)
    return jnp.concatenate([cos, cos], axis=-1), jnp.concatenate([-sin, sin], axis=-1)


def _decay_tables(chunk):
    steps = (jnp.arange(chunk, dtype=F32) + 1.0)[None, :, None]
    lgam = jnp.asarray(LOG_GAMMA, F32)[:, None, None]
    gq = jnp.broadcast_to(jnp.exp(steps * lgam), (N_HEADS, chunk, DV))
    gk = jnp.broadcast_to(jnp.exp(-steps * lgam) * (DV ** -0.5), (N_HEADS, chunk, DV))
    return gq, gk


def kernel(x_prompt, x_sample, state_gla, state_ret, norm_g, w1_gu, w1_down, w2_gu, w2_down,
           w_in, w_a2, b_a, gn_gla, gn_ret, w_out):
    batch, seq, _ = x_prompt.shape
    n_seq, n_tok, _ = x_sample.shape
    n_prompt = batch * seq
    n_sample = n_seq * n_tok
    assert n_prompt % FFN_SUB_TILE == 0 and n_sample % FFN_SUB_TILE == 0
    assert seq % (MIX_SUB_TILE * MIX_SUB_TILES_PER_STEP) == 0
    assert n_seq % SAMPLE_SEQ_BLOCK == 0 and n_tok in (1, 2, 4, 8)
    mix_tile = MIX_SUB_TILE * MIX_SUB_TILES_PER_STEP

    w1_gu_b, w1_dn_b = w1_gu.astype(BF16), w1_down.astype(BF16)
    w2_gu_b, w2_dn_b = w2_gu.astype(BF16), w2_down.astype(BF16)
    w_in_b = w_in.astype(BF16)
    wa_b = w_in_b[..., :A_COLS]
    wr_b = w_in_b[..., A_COLS + GATE_RANK:]
    wl_b = jnp.pad(w_in_b[..., A_COLS:A_COLS + GATE_RANK], ((0, 0), (0, 0), (0, AL_PAD - GATE_RANK)))
    wa2_b = jnp.pad(w_a2.astype(BF16), ((0, 0), (0, AL_PAD - GATE_RANK), (0, 0)))
    w_out_b = w_out.astype(BF16)
    b_a3 = b_a[:, None, :]
    gn = jnp.concatenate([gn_gla, gn_ret], axis=-1)[:, None, :]

    key_bound = (math.sqrt(D_MODEL) * jnp.max(jnp.abs(norm_g[:, 2, :]), axis=-1)
                 * jnp.max(jnp.sqrt(jnp.sum(jnp.square(w_in[:, :, KA:KA + 256]), axis=1)), axis=-1))

    cos_p, sin_p = _rope_tables(jnp.arange(seq, dtype=F32))
    cos_s, sin_s = _rope_tables(PAST_LEN + jnp.arange(n_tok, dtype=F32))
    gq_p, gk_p = _decay_tables(MIX_SUB_TILE)
    gq_s, gk_s = _decay_tables(n_tok)

    xp = x_prompt.reshape(n_prompt, D_MODEL)
    xs = x_sample.reshape(n_sample, D_MODEL)
    gla_p, ret_p, gla_s, ret_s = [], [], [], []
    for l in range(DEPTH):
        xp, xs = _ffn(xp, xs, l, norm_g, w1_gu_b, w1_dn_b, 0)
        pa, pr, lg = _proj_sample(xs, l, norm_g, wa_b, wr_b, wl_b, wa2_b, b_a3)
        o2, sg_s, sr_s = _core_sample(pa, pr, lg, l, n_tok, cos_s, sin_s, gq_s, gk_s, state_gla, state_ret)
        xs = _out_sample(o2, pa, pr, l, gn, w_out_b, norm_g, xs)
        mixer = functools.partial(_mixer_prompt, batch=batch, seq=seq, tt=mix_tile, layer=l, norm_g=norm_g,
                                  wa=wa_b, wr=wr_b, wl=wl_b, wa2=wa2_b, ba=b_a3, gn=gn, wout=w_out_b,
                                  cos2=cos_p, sin2=sin_p, gq=gq_p, gk=gk_p)
        fast = mixer(xp, robust=False)
        bounded = jnp.logical_and(jnp.max(fast[3]) < GLA_SAFE_GATE_ABS, key_bound[l] < GLA_SAFE_KEY_ABS)
        xp, sg_p, sr_p = lax.cond(bounded, lambda x, kept: kept, lambda x, kept: mixer(x, robust=True)[:3],
                                  xp, fast[:3])
        xp, xs = _ffn(xp, xs, l, norm_g, w2_gu_b, w2_dn_b, 4)
        gla_p.append(sg_p)
        ret_p.append(sr_p)
        gla_s.append(sg_s)
        ret_s.append(sr_s)
    yp, ys = xp, xs

    return (yp.reshape(batch, seq, D_MODEL), ys.reshape(n_seq, n_tok, D_MODEL),
            jnp.stack(gla_p), jnp.stack(ret_p), jnp.stack(gla_s), jnp.stack(ret_s))
```

```python
import functools
import math

import jax
import jax.numpy as jnp
from jax import lax
from jax.experimental import pallas as pl
from jax.experimental.pallas import tpu as pltpu

F32 = jnp.float32
BF16 = jnp.bfloat16

D_MODEL = 1024
DEPTH = 4
PAST_LEN = 16384
N_HEADS = 4
DK_GLA = 64
DV = 128
GATE_RANK = 16
GATE_TEMP = 16.0
D_FF = 2816
ROPE_BASE = 10000.0
EPS = 1e-6
GLA_CHUNK = 64

A_COLS = 1536
R_COLS = 2048
QA, KA, VA, GA = 0, 256, 512, 1024
QR, KR, VR, GR = 0, 512, 1024, 1536
AL_PAD = 128

FFN_SUB_TILE = 512
FFN_FF_TILE = 256
MIX_SUB_TILE = 256
MIX_SUB_TILES_PER_STEP = 1
SAMPLE_SEQ_BLOCK = 16
VMEM_LIMIT_BYTES = 56 * 1024 * 1024
GLA_SAFE_GATE_ABS = 14.0
GLA_SAFE_KEY_ABS = 1e9

LOG_GAMMA = tuple(math.log1p(-(2.0 ** (-5.0 - h))) for h in range(N_HEADS))


def _mm(a, b):
    return jnp.dot(a, b, preferred_element_type=F32)


def _mm_nt(a, b):
    return lax.dot_general(a, b, (((1,), (1,)), ((), ())), preferred_element_type=F32)


def _mm_tn(a, b):
    return lax.dot_general(a, b, (((0,), (0,)), ((), ())), preferred_element_type=F32)


def _rms(x, g):
    ms = jnp.mean(x * x, axis=-1, keepdims=True)
    return x * lax.rsqrt(ms + EPS) * g


def _silu(x):
    return x * (1.0 / (1.0 + jnp.exp(-x)))


def _log_sigmoid(x):
    return jnp.minimum(x, 0.0) - jnp.log1p(jnp.exp(-jnp.abs(x)))


def _split_bf16(x):
    hi = x.astype(BF16)
    lo = (x - hi.astype(F32)).astype(BF16)
    return hi, lo


def _iota(shape, dim):
    return lax.broadcasted_iota(jnp.int32, shape, dim)


def _rotary(x, cos2, sin2):
    return x * cos2 + pltpu.roll(x, DV // 2, axis=1) * sin2


def _head_norm_gate(o, gn, gate):
    outs = []
    for h in range(N_HEADS):
        oh = o[:, h * DV:(h + 1) * DV]
        ms = jnp.mean(oh * oh, axis=-1, keepdims=True)
        outs.append(oh * lax.rsqrt(ms + EPS))
    return jnp.concatenate(outs, axis=1) * gn * _silu(gate)


def _first_of_block(x, block, row):
    bit = 1
    while bit < block:
        x = jnp.where(jnp.bitwise_and(row, bit) != 0, pltpu.roll(x, bit, axis=0), x)
        bit *= 2
    return x


def _stack_heads(x, k_head_mask_b):
    return jnp.concatenate([x.astype(BF16)] * N_HEADS, axis=0) * k_head_mask_b


def _gla_level_operands(qv, kv, bv, lgv, k_head_mask_b):
    n = qv.shape[0]
    row = _iota((n, 256), 0)
    pos16, pos4, pos1 = jnp.right_shift(row, 4), jnp.bitwise_and(jnp.right_shift(row, 2), 3), jnp.bitwise_and(row, 3)
    b_first16 = jnp.concatenate([jnp.broadcast_to(bv[16 * g:16 * g + 1, :], (16, 256))
                                 for g in range(n // 16)], axis=0)
    q1 = qv * jnp.exp(bv - b_first16)
    q2 = qv * jnp.exp(bv - _first_of_block(bv, 4, row))
    k1 = [jnp.where(row < 16 * j, kv * jnp.exp(bv[16 * j:16 * j + 1, :] - bv), 0.0) for j in range(1, 4)]
    k2 = []
    for j in range(1, 4):
        ref = jnp.concatenate([jnp.broadcast_to(bv[16 * g + 4 * j:16 * g + 4 * j + 1, :], (16, 256))
                               for g in range(n // 16)], axis=0)
        k2.append(jnp.where(jnp.bitwise_and(row, 15) < 4 * j, kv * jnp.exp(ref - bv), 0.0))
    f1 = pltpu.roll(lgv, n - 1, axis=0)
    f2 = f1 + pltpu.roll(lgv, n - 2, axis=0)
    f3 = f2 + pltpu.roll(lgv, n - 3, axis=0)
    ahead = [kv, kv * jnp.exp(f1), kv * jnp.exp(f2), kv * jnp.exp(f3)]
    k3 = []
    for j in range(4):
        k_j = jnp.zeros_like(kv)
        for p in range(j + 1):
            k_j = jnp.where(pos1 == p, ahead[j - p], k_j)
        k3.append(k_j)
    lhs = []
    for q_lvl, pos, first_slot in ((q1, pos16, 1), (q2, pos4, 1), (qv, pos1, 0)):
        q_b = q_lvl.astype(BF16)
        lhs += [q_b * jnp.where(pos == j, 1.0, 0.0).astype(BF16) for j in range(first_slot, 4)]
    rhs = [_stack_heads(k_j, k_head_mask_b) for k_j in k1 + k2 + k3]
    return jnp.concatenate(lhs, axis=1), jnp.concatenate(rhs, axis=1)


def _gla_level_scores(lhs, rhs):
    n, w = lhs.shape[0], N_HEADS * DK_GLA
    s1 = _mm_nt(lhs[:, 0:3 * w], rhs[:, 0:3 * w])
    s2 = _mm_nt(lhs[:, 3 * w:6 * w], rhs[:, 3 * w:6 * w])
    s3 = _mm_nt(lhs[:, 6 * w:10 * w], rhs[:, 6 * w:10 * w])
    t_idx = _iota((n, N_HEADS * n), 0)
    s_idx = jnp.bitwise_and(_iota((n, N_HEADS * n), 1), n - 1)
    same16 = jnp.right_shift(t_idx, 4) == jnp.right_shift(s_idx, 4)
    same4 = jnp.right_shift(t_idx, 2) == jnp.right_shift(s_idx, 2)
    return jnp.where(same4, s3, jnp.where(same16, s2, s1))


def _layer_spec(arr, layer, n_grid_axes):
    zeros = (0,) * (arr.ndim - 1)
    if n_grid_axes == 1:
        return pl.BlockSpec((None,) + arr.shape[1:], lambda i: (layer,) + zeros)
    return pl.BlockSpec((None,) + arr.shape[1:], lambda i, j: (layer,) + zeros)


def _params(n_axes):
    return pltpu.CompilerParams(dimension_semantics=("arbitrary",) * n_axes,
                                vmem_limit_bytes=VMEM_LIMIT_BYTES)


def _ffn_rows(x, ng_ref, wgu_ref, wdn_ref, pre_row):
    xn = _rms(x, ng_ref[pre_row:pre_row + 1, :]).astype(BF16)
    acc = jnp.zeros(x.shape, F32)
    for c in range(D_FF // FFN_FF_TILE):
        lo = c * FFN_FF_TILE
        gate = _mm(xn, wgu_ref[:, lo:lo + FFN_FF_TILE])
        up = _mm(xn, wgu_ref[:, D_FF + lo:D_FF + lo + FFN_FF_TILE])
        act = (_silu(gate) * up).astype(BF16)
        acc = acc + _mm(act, wdn_ref[lo:lo + FFN_FF_TILE, :])
    return x + 0.5 * _rms(acc, ng_ref[pre_row + 1:pre_row + 2, :])


def _ffn_kernel(xp_ref, xs_ref, ng_ref, wgu_ref, wdn_ref, yp_ref, ys_ref, *, pre_row, n_prompt_tiles):
    is_prompt = pl.program_id(0) < n_prompt_tiles
    x = jnp.where(is_prompt, xp_ref[...], xs_ref[...])
    y = _ffn_rows(x, ng_ref, wgu_ref, wdn_ref, pre_row)

    @pl.when(is_prompt)
    def _():
        yp_ref[...] = y

    @pl.when(jnp.logical_not(is_prompt))
    def _():
        ys_ref[...] = y


def _ffn(xp, xs, layer, norm_g, wgu, wdn, pre_row):
    tm = FFN_SUB_TILE
    n_p, n_s = xp.shape[0] // tm, xs.shape[0] // tm
    specs = [pl.BlockSpec((tm, D_MODEL), lambda i: (jnp.minimum(i, n_p - 1), 0)),
             pl.BlockSpec((tm, D_MODEL), lambda i: (jnp.maximum(i - n_p, 0), 0))]
    weights = (norm_g, wgu, wdn)
    return pl.pallas_call(
        functools.partial(_ffn_kernel, pre_row=pre_row, n_prompt_tiles=n_p),
        out_shape=(jax.ShapeDtypeStruct(xp.shape, F32), jax.ShapeDtypeStruct(xs.shape, F32)),
        grid=(n_p + n_s,),
        in_specs=specs + [_layer_spec(w, layer, 1) for w in weights],
        out_specs=specs,
        compiler_params=_params(1),
        name="ffn",
    )(xp, xs, *weights)


def _mixer_rows(x, cos2, sin2, sgt, sr, consts, robust, ng_ref, wa_ref, wr_ref, wl_ref, wa2_ref,
                ba_ref, gn_ref, wout_ref, gq_ref, gk_ref):
    tri, k_head_mask, v_head_mask, causal, bd_mask, causal_r = consts
    tt = x.shape[0]
    xn = _rms(x, ng_ref[2:3, :]).astype(BF16)

    alo = _mm(xn, wl_ref[...]).astype(BF16)
    gate_pre = _mm(alo, wa2_ref[...])
    q = _mm(xn, wa_ref[:, QA:QA + 256]) * (DK_GLA ** -0.5)
    k = _mm(xn, wa_ref[:, KA:KA + 256])
    v = _mm(xn, wa_ref[:, VA:VA + 512]).astype(BF16)
    gate_pre = gate_pre + ba_ref[...]
    gate_abs = jnp.max(jnp.abs(gate_pre), axis=0, keepdims=True)
    gate_range = jnp.broadcast_to(jnp.maximum(gate_abs[:, 0:128], gate_abs[:, 128:256]), (8, 128))
    lg = _log_sigmoid(gate_pre) * (1.0 / GATE_TEMP)
    lg_hi, lg_lo = _split_bf16(lg)
    bcum = _mm(tri, lg_hi) + _mm(tri, lg_lo)
    qr = _mm(xn, wr_ref[:, QR:QR + 512])
    kr = _mm(xn, wr_ref[:, KR:KR + 512])
    b_end = bcum[tt - 1:tt, :]

    o_inter = _mm_nt((q * jnp.exp(bcum)).astype(BF16), sgt.astype(BF16))
    k_end = (k * jnp.exp(b_end - bcum)).astype(BF16)
    sgt = jnp.exp(b_end) * sgt + bd_mask * _mm_tn(v, k_end)
    vr = _mm(xn, wr_ref[:, VR:VR + 512]).astype(BF16)

    n_chunks = tt // GLA_CHUNK
    chunk = lambda j: slice(j * GLA_CHUNK, (j + 1) * GLA_CHUNK)
    chunk_last = lambda j: bcum[(j + 1) * GLA_CHUNK - 1:(j + 1) * GLA_CHUNK, :]
    vstacks = [jnp.where(v_head_mask, jnp.concatenate([v[chunk(j)]] * N_HEADS, axis=0), 0.0)
               for j in range(n_chunks)]

    def gla_cross_scores(cj):
        later = slice((cj + 1) * GLA_CHUNK, tt)
        q_later = (q[later] * jnp.exp(bcum[later] - chunk_last(cj))).astype(BF16)
        kc = k[chunk(cj)] * jnp.exp(chunk_last(cj) - bcum[chunk(cj)])
        return _mm_nt(q_later, _stack_heads(kc, k_head_mask))

    def gla_cross_values(cj, scores):
        return _mm(scores.astype(BF16), vstacks[cj])

    def gla_same_scores(ci):
        if robust:
            return _gla_level_scores(*_gla_level_operands(q[chunk(ci)], k[chunk(ci)], bcum[chunk(ci)],
                                                          lg[chunk(ci)], k_head_mask))
        ref = chunk_last(ci - 1) if ci > 0 else jnp.zeros((1, 256), F32)
        qc = (q[chunk(ci)] * jnp.exp(bcum[chunk(ci)] - ref)).astype(BF16)
        kc = k[chunk(ci)] * jnp.exp(ref - bcum[chunk(ci)])
        return jnp.where(causal, _mm_nt(qc, _stack_heads(kc, k_head_mask)), 0.0)

    def gla_same_values(ci, scores):
        return _mm(scores.astype(BF16), vstacks[ci])

    def ret_scores(h):
        hs = slice(h * DV, (h + 1) * DV)
        qh = (_rotary(qr[:, hs], cos2, sin2) * gq_ref[h]).astype(BF16)
        kh = (_rotary(kr[:, hs], cos2, sin2) * gk_ref[h]).astype(BF16)
        vh = vr[:, hs]
        a = _mm_nt(qh, kh)
        o_state = _mm(qh, sr[h].astype(BF16))
        s_new = math.exp(tt * LOG_GAMMA[h]) * (sr[h] + _mm_tn(kh, vh))
        return a, o_state, s_new, vh

    def ret_values(h, staged):
        a, o_state, _, vh = staged
        return o_state + _mm(jnp.where(causal_r, a, 0.0).astype(BF16), vh)

    stage1 = {"ret": ret_scores, "same": gla_same_scores, "cross": gla_cross_scores}
    stage2 = {"ret": ret_values, "same": gla_same_values, "cross": gla_cross_values}
    items = [("cross", i) for i in range(n_chunks - 1)]
    for i in range(max(n_chunks, N_HEADS)):
        if i < n_chunks:
            items.append(("same", i))
        if i < N_HEADS:
            items.append(("ret", i))
    gate_cols = [(wa_ref, GA), (wa_ref, GA + 256), (wr_ref, GR), (wr_ref, GR + 256)]
    gate_parts, staged, done = [], {}, {}
    depth = 1
    for step in range(len(items) + depth):
        if step < len(items):
            kind, i = items[step]
            staged[(kind, i)] = stage1[kind](i)
        if step % 2 == 1 and len(gate_parts) < len(gate_cols):
            w_ref, off = gate_cols[len(gate_parts)]
            gate_parts.append(_mm(xn, w_ref[:, off:off + 256]))
        if step >= depth:
            kind, i = items[step - depth]
            done[(kind, i)] = stage2[kind](i, staged[(kind, i)])
    while len(gate_parts) < len(gate_cols):
        w_ref, off = gate_cols[len(gate_parts)]
        gate_parts.append(_mm(xn, w_ref[:, off:off + 256]))
    gate_a = jnp.concatenate(gate_parts[0:2], axis=1)
    gate_r = jnp.concatenate(gate_parts[2:4], axis=1)

    o_a = o_inter + jnp.concatenate([done[("same", ci)] for ci in range(n_chunks)], axis=0)
    for cj in range(n_chunks - 1):
        o_a = o_a + jnp.concatenate(
            [jnp.zeros(((cj + 1) * GLA_CHUNK, N_HEADS * DV), F32), done[("cross", cj)]], axis=0)
    o_r = jnp.concatenate([done[("ret", h)] for h in range(N_HEADS)], axis=1)
    sr_new = [staged[("ret", h)][2] for h in range(N_HEADS)]

    y_a = _head_norm_gate(o_a, gn_ref[:, 0:512], gate_a)
    y_r = _head_norm_gate(o_r, gn_ref[:, 512:1024], gate_r)
    y = jnp.concatenate([y_a, y_r], axis=1).astype(BF16)
    return x + _rms(_mm(y, wout_ref[...]), ng_ref[3:4, :]), sgt, sr_new, gate_range


def _mixer_prompt_kernel(x_ref, ng_ref, wa_ref, wr_ref, wl_ref, wa2_ref, ba_ref, gn_ref, wout_ref,
                         cos_ref, sin_ref, gq_ref, gk_ref, o_ref, sg_ref, sr_ref, range_ref, sgt_s, sr_s, *, robust):
    t_idx = pl.program_id(1)

    @pl.when(t_idx == 0)
    def _():
        sgt_s[...] = jnp.zeros(sgt_s.shape, F32)
        sr_s[...] = jnp.zeros(sr_s.shape, F32)

    sub = MIX_SUB_TILE
    rows4 = N_HEADS * GLA_CHUNK
    r = _iota((sub, sub), 0)
    c = _iota((sub, sub), 1)
    consts = (
        jnp.where(c <= r, 1.0, 0.0).astype(BF16),
        jnp.where(jnp.right_shift(_iota((rows4, 256), 0), 6)
                  == jnp.right_shift(_iota((rows4, 256), 1), 6), 1.0, 0.0).astype(BF16),
        jnp.right_shift(_iota((rows4, 512), 0), 6) == jnp.right_shift(_iota((rows4, 512), 1), 7),
        jnp.bitwise_and(_iota((GLA_CHUNK, rows4), 1), GLA_CHUNK - 1) <= _iota((GLA_CHUNK, rows4), 0),
        jnp.where(jnp.right_shift(_iota((512, 256), 0), 7)
                  == jnp.right_shift(_iota((512, 256), 1), 6), 1.0, 0.0),
        c <= r,
    )
    sgt = sgt_s[...]
    sr = [sr_s[h] for h in range(N_HEADS)]
    for s in range(x_ref.shape[0] // sub):
        rs = slice(s * sub, (s + 1) * sub)
        y, sgt, sr, gate_range = _mixer_rows(x_ref[rs, :], cos_ref[rs, :], sin_ref[rs, :], sgt, sr, consts,
                                             robust, ng_ref, wa_ref, wr_ref, wl_ref, wa2_ref, ba_ref, gn_ref,
                                             wout_ref, gq_ref, gk_ref)
        o_ref[rs, :] = y
        first = jnp.logical_and(pl.program_id(0) == 0, jnp.logical_and(t_idx == 0, s == 0))
        range_ref[...] = jnp.where(first, gate_range, jnp.maximum(range_ref[...], gate_range))
    sgt_s[...] = sgt
    for h in range(N_HEADS):
        sr_s[h] = sr[h]

    @pl.when(t_idx == pl.num_programs(1) - 1)
    def _():
        sg = sgt_s[...].T
        for h in range(N_HEADS):
            sg_ref[0, h] = sg[h * DK_GLA:(h + 1) * DK_GLA, h * DV:(h + 1) * DV]
        sr_ref[0] = sr_s[...]


def _mixer_prompt(x_prompt, robust, batch, seq, tt, layer, norm_g, wa, wr, wl, wa2, ba, gn, wout, cos2, sin2,
                  gq, gk):
    n_t = seq // tt
    x_map = lambda b, t: (b * n_t + t, 0)
    const3 = lambda b, t: (0, 0, 0)
    out_shapes = (
        jax.ShapeDtypeStruct((batch * seq, D_MODEL), F32),
        jax.ShapeDtypeStruct((batch, N_HEADS, DK_GLA, DV), F32),
        jax.ShapeDtypeStruct((batch, N_HEADS, DV, DV), F32),
        jax.ShapeDtypeStruct((8, 128), F32),
    )
    weights = (norm_g, wa, wr, wl, wa2, ba, gn, wout)
    return pl.pallas_call(
        functools.partial(_mixer_prompt_kernel, robust=robust),
        out_shape=out_shapes,
        grid=(batch, n_t),
        in_specs=[pl.BlockSpec((tt, D_MODEL), x_map)]
        + [_layer_spec(w, layer, 2) for w in weights]
        + [pl.BlockSpec((tt, DV), lambda b, t: (t, 0)),
           pl.BlockSpec((tt, DV), lambda b, t: (t, 0)),
           pl.BlockSpec(gq.shape, const3),
           pl.BlockSpec(gk.shape, const3)],
        out_specs=(
            pl.BlockSpec((tt, D_MODEL), x_map),
            pl.BlockSpec((1, N_HEADS, DK_GLA, DV), lambda b, t: (b, 0, 0, 0)),
            pl.BlockSpec((1, N_HEADS, DV, DV), lambda b, t: (b, 0, 0, 0)),
            pl.BlockSpec((8, 128), lambda b, t: (0, 0)),
        ),
        scratch_shapes=[
            pltpu.VMEM((N_HEADS * DV, N_HEADS * DK_GLA), F32),
            pltpu.VMEM((N_HEADS, DV, DV), F32),
        ],
        compiler_params=_params(2),
        name="mixer_prompt_levels" if robust else "mixer_prompt",
    )(x_prompt, *weights, cos2, sin2, gq, gk)


def _proj_sample_kernel(x_ref, ng_ref, wa_ref, wr_ref, wl_ref, wa2_ref, ba_ref, pa_ref, pr_ref, lg_ref):
    xn = _rms(x_ref[...], ng_ref[2:3, :]).astype(BF16)
    pa_ref[...] = _mm(xn, wa_ref[...])
    pr_ref[...] = _mm(xn, wr_ref[...])
    alo = _mm(xn, wl_ref[...]).astype(BF16)
    lg_ref[...] = _log_sigmoid(_mm(alo, wa2_ref[...]) + ba_ref[...]) * (1.0 / GATE_TEMP)


def _proj_sample(x_sample, layer, norm_g, wa, wr, wl, wa2, ba):
    n_rows = x_sample.shape[0]
    const = lambda i: (0, 0)
    weights = (norm_g, wa, wr, wl, wa2, ba)
    return pl.pallas_call(
        _proj_sample_kernel,
        out_shape=(jax.ShapeDtypeStruct((n_rows, A_COLS), F32),
                   jax.ShapeDtypeStruct((n_rows, R_COLS), F32),
                   jax.ShapeDtypeStruct((n_rows, 256), F32)),
        grid=(1,),
        in_specs=[pl.BlockSpec((n_rows, D_MODEL), const)] + [_layer_spec(w, layer, 1) for w in weights],
        out_specs=(pl.BlockSpec((n_rows, A_COLS), const),
                   pl.BlockSpec((n_rows, R_COLS), const),
                   pl.BlockSpec((n_rows, 256), const)),
        compiler_params=_params(1),
        name="proj_sample",
    )(x_sample, *weights)


def _core_sample_kernel(pa_ref, pr_ref, lg_ref, cos_ref, sin_ref, gq_ref, gk_ref, sg_in, sr_in,
                        o_ref, sg_out, sr_out, *, n_tok):
    n_seq = sg_in.shape[0]
    rows4 = N_HEADS * n_tok
    tok_shift = n_tok.bit_length() - 1
    k_head_mask = jnp.where(jnp.right_shift(_iota((rows4, 256), 0), tok_shift)
                            == jnp.right_shift(_iota((rows4, 256), 1), 6), 1.0, 0.0).astype(BF16)
    v_head_mask = (jnp.right_shift(_iota((rows4, 512), 0), tok_shift)
                   == jnp.right_shift(_iota((rows4, 512), 1), 7))
    tok_row = _iota((n_tok, 256), 0)
    causal_r = _iota((n_tok, n_tok), 1) <= _iota((n_tok, n_tok), 0)
    ones = jnp.ones((2 * n_tok, DV), BF16)
    cos2 = cos_ref[...]
    sin2 = sin_ref[...]

    def one_seq(n, pa, pr, lg):
        b_rows = [lg[0:1]]
        for t in range(1, n_tok):
            b_rows.append(b_rows[-1] + lg[t:t + 1])
        bc = jnp.concatenate(b_rows, axis=0)
        b_last = b_rows[-1]
        k = pa[:, KA:KA + 256]
        v = pa[:, VA:VA + 512].astype(BF16)
        q = pa[:, QA:QA + 256] * (DK_GLA ** -0.5)
        qt = (q * jnp.exp(bc)).astype(BF16)
        kp = (k * jnp.exp(b_last - bc)).astype(BF16)
        s_old = sg_in[n].reshape(N_HEADS * DK_GLA, DV)
        s_old_b = s_old.astype(BF16)
        s_bd = jnp.concatenate(
            [jnp.where(jnp.right_shift(_iota((N_HEADS * DK_GLA, DV), 0), 6) == h, s_old_b, 0.0)
             for h in range(N_HEADS)], axis=1)
        vstack = jnp.where(v_head_mask, jnp.concatenate([v] * N_HEADS, axis=0), 0.0)
        lhs = [jnp.where(tok_row == j, q, 0.0).astype(BF16) for j in range(n_tok)]
        rhs = [_stack_heads(jnp.where(tok_row <= j, k * jnp.exp(b_rows[j] - bc), 0.0), k_head_mask)
               for j in range(n_tok)]
        a = _mm_nt(jnp.concatenate(lhs, axis=1), jnp.concatenate(rhs, axis=1)).astype(BF16)
        o_a = _mm(qt, s_bd) + _mm(a, vstack)
        lg_hi, lg_lo = _split_bf16(lg)
        b_last_t = _mm_tn(jnp.concatenate([lg_hi, lg_lo], axis=0), ones)
        u = _mm_tn(kp, v)
        u_d = jnp.concatenate(
            [u[h * DK_GLA:(h + 1) * DK_GLA, h * DV:(h + 1) * DV] for h in range(N_HEADS)], axis=0)
        sg_out[n] = (jnp.exp(b_last_t) * s_old + u_d).reshape(N_HEADS, DK_GLA, DV)
        vr = pr[:, VR:VR + 512].astype(BF16)
        o_r_heads = []
        for h in range(N_HEADS):
            hs = slice(h * DV, (h + 1) * DV)
            qh = (_rotary(pr[:, QR + h * DV:QR + (h + 1) * DV], cos2, sin2) * gq_ref[h]).astype(BF16)
            kh = (_rotary(pr[:, KR + h * DV:KR + (h + 1) * DV], cos2, sin2) * gk_ref[h]).astype(BF16)
            vh = vr[:, hs]
            ar = jnp.where(causal_r, _mm_nt(qh, kh), 0.0).astype(BF16)
            s_r = sr_in[n, h]
            o_r_heads.append(_mm(qh, s_r.astype(BF16)) + _mm(ar, vh))
            sr_out[n, h] = math.exp(n_tok * LOG_GAMMA[h]) * (s_r + _mm_tn(kh, vh))
        return jnp.concatenate([o_a] + o_r_heads, axis=1)

    seq_per_iter = 8 // n_tok

    def body(i, carry):
        rows = pl.ds(pl.multiple_of(i * 8, 8), 8)
        pa8, pr8, lg8 = pa_ref[rows, :], pr_ref[rows, :], lg_ref[rows, :]
        outs = []
        for j in range(seq_per_iter):
            sl = slice(j * n_tok, (j + 1) * n_tok)
            outs.append(one_seq(i * seq_per_iter + j, pa8[sl], pr8[sl], lg8[sl]))
        o_ref[rows, :] = jnp.concatenate(outs, axis=0)
        return carry

    lax.fori_loop(0, n_seq // seq_per_iter, body, 0, unroll=4)


def _core_sample(pa, pr, lg, layer, n_tok, cos2, sin2, gq, gk, state_gla, state_ret):
    n_seq = state_gla.shape[1]
    nb = SAMPLE_SEQ_BLOCK
    rows = nb * n_tok
    const2 = lambda i: (0, 0)
    const3 = lambda i: (0, 0, 0)
    out_shapes = (
        jax.ShapeDtypeStruct((n_seq * n_tok, D_MODEL), F32),
        jax.ShapeDtypeStruct(state_gla.shape[1:], F32),
        jax.ShapeDtypeStruct(state_ret.shape[1:], F32),
    )
    return pl.pallas_call(
        functools.partial(_core_sample_kernel, n_tok=n_tok),
        out_shape=out_shapes,
        grid=(n_seq // nb,),
        in_specs=[
            pl.BlockSpec((rows, A_COLS), lambda i: (i, 0)),
            pl.BlockSpec((rows, R_COLS), lambda i: (i, 0)),
            pl.BlockSpec((rows, 256), lambda i: (i, 0)),
            pl.BlockSpec(cos2.shape, const2),
            pl.BlockSpec(sin2.shape, const2),
            pl.BlockSpec(gq.shape, const3),
            pl.BlockSpec(gk.shape, const3),
            pl.BlockSpec((None, nb, N_HEADS, DK_GLA, DV), lambda i: (layer, i, 0, 0, 0)),
            pl.BlockSpec((None, nb, N_HEADS, DV, DV), lambda i: (layer, i, 0, 0, 0)),
        ],
        out_specs=(
            pl.BlockSpec((rows, D_MODEL), lambda i: (i, 0)),
            pl.BlockSpec((nb, N_HEADS, DK_GLA, DV), lambda i: (i, 0, 0, 0)),
            pl.BlockSpec((nb, N_HEADS, DV, DV), lambda i: (i, 0, 0, 0)),
        ),
        compiler_params=_params(1),
        name="core_sample",
    )(pa, pr, lg, cos2, sin2, gq, gk, state_gla, state_ret)


def _out_sample_kernel(o_ref, ga_ref, gr_ref, gn_ref, wout_ref, ng_ref, x_ref, y_ref):
    o = o_ref[...]
    y_a = _head_norm_gate(o[:, 0:512], gn_ref[:, 0:512], ga_ref[...])
    y_r = _head_norm_gate(o[:, 512:1024], gn_ref[:, 512:1024], gr_ref[...])
    y = jnp.concatenate([y_a, y_r], axis=1).astype(BF16)
    y_ref[...] = x_ref[...] + _rms(_mm(y, wout_ref[...]), ng_ref[3:4, :])


def _out_sample(o2, pa, pr, layer, gn, wout, norm_g, x_sample):
    n_rows = o2.shape[0]
    const = lambda i: (0, 0)
    return pl.pallas_call(
        _out_sample_kernel,
        out_shape=jax.ShapeDtypeStruct((n_rows, D_MODEL), F32),
        grid=(1,),
        in_specs=[
            pl.BlockSpec(o2.shape, const),
            pl.BlockSpec((n_rows, 512), lambda i: (0, GA // 512)),
            pl.BlockSpec((n_rows, 512), lambda i: (0, GR // 512)),
            _layer_spec(gn, layer, 1),
            _layer_spec(wout, layer, 1),
            _layer_spec(norm_g, layer, 1),
            pl.BlockSpec((n_rows, D_MODEL), const),
        ],
        out_specs=pl.BlockSpec((n_rows, D_MODEL), const),
        compiler_params=_params(1),
        name="out_sample",
    )(o2, pa, pr, gn, wout, norm_g, x_sample)


def _rope_tables(pos):
    half = DV // 2
    inv_freq = ROPE_BASE ** (-jnp.arange(half, dtype=F32) / half)
    ang = pos[:, None] * inv_freq[None, :]
    cos, sin = jnp.cos(ang), jnp.sin(ang)
    return jnp.concatenate([cos, cos], axis=-1), jnp.concatenate([-sin, sin], axis=-1)


def _decay_tables(chunk):
    steps = (jnp.arange(chunk, dtype=F32) + 1.0)[None, :, None]
    lgam = jnp.asarray(LOG_GAMMA, F32)[:, None, None]
    gq = jnp.broadcast_to(jnp.exp(steps * lgam), (N_HEADS, chunk, DV))
    gk = jnp.broadcast_to(jnp.exp(-steps * lgam) * (DV ** -0.5), (N_HEADS, chunk, DV))
    return gq, gk


def kernel(x_prompt, x_sample, state_gla, state_ret, norm_g, w1_gu, w1_down, w2_gu, w2_down,
           w_in, w_a2, b_a, gn_gla, gn_ret, w_out):
    batch, seq, _ = x_prompt.shape
    n_seq, n_tok, _ = x_sample.shape
    n_prompt = batch * seq
    n_sample = n_seq * n_tok
    assert n_prompt % FFN_SUB_TILE == 0 and n_sample % FFN_SUB_TILE == 0
    assert seq % (MIX_SUB_TILE * MIX_SUB_TILES_PER_STEP) == 0
    assert n_seq % SAMPLE_SEQ_BLOCK == 0 and n_tok in (1, 2, 4, 8)
    mix_tile = MIX_SUB_TILE * MIX_SUB_TILES_PER_STEP

    w1_gu_b, w1_dn_b = w1_gu.astype(BF16), w1_down.astype(BF16)
    w2_gu_b, w2_dn_b = w2_gu.astype(BF16), w2_down.astype(BF16)
    w_in_b = w_in.astype(BF16)
    wa_b = w_in_b[..., :A_COLS]
    wr_b = w_in_b[..., A_COLS + GATE_RANK:]
    wl_b = jnp.pad(w_in_b[..., A_COLS:A_COLS + GATE_RANK], ((0, 0), (0, 0), (0, AL_PAD - GATE_RANK)))
    wa2_b = jnp.pad(w_a2.astype(BF16), ((0, 0), (0, AL_PAD - GATE_RANK), (0, 0)))
    w_out_b = w_out.astype(BF16)
    b_a3 = b_a[:, None, :]
    gn = jnp.concatenate([gn_gla, gn_ret], axis=-1)[:, None, :]

    key_bound = (math.sqrt(D_MODEL) * jnp.max(jnp.abs(norm_g[:, 2, :]), axis=-1)
                 * jnp.max(jnp.sqrt(jnp.sum(jnp.square(w_in[:, :, KA:KA + 256]), axis=1)), axis=-1))

    cos_p, sin_p = _rope_tables(jnp.arange(seq, dtype=F32))
    cos_s, sin_s = _rope_tables(PAST_LEN + jnp.arange(n_tok, dtype=F32))
    gq_p, gk_p = _decay_tables(MIX_SUB_TILE)
    gq_s, gk_s = _decay_tables(n_tok)

    xp = x_prompt.reshape(n_prompt, D_MODEL)
    xs = x_sample.reshape(n_sample, D_MODEL)
    gla_p, ret_p, gla_s, ret_s = [], [], [], []
    for l in range(DEPTH):
        xp, xs = _ffn(xp, xs, l, norm_g, w1_gu_b, w1_dn_b, 0)
        pa, pr, lg = _proj_sample(xs, l, norm_g, wa_b, wr_b, wl_b, wa2_b, b_a3)
        o2, sg_s, sr_s = _core_sample(pa, pr, lg, l, n_tok, cos_s, sin_s, gq_s, gk_s, state_gla, state_ret)
        xs = _out_sample(o2, pa, pr, l, gn, w_out_b, norm_g, xs)
        mixer = functools.partial(_mixer_prompt, batch=batch, seq=seq, tt=mix_tile, layer=l, norm_g=norm_g,
                                  wa=wa_b, wr=wr_b, wl=wl_b, wa2=wa2_b, ba=b_a3, gn=gn, wout=w_out_b,
                                  cos2=cos_p, sin2=sin_p, gq=gq_p, gk=gk_p)
        fast = mixer(xp, robust=False)
        bounded = jnp.logical_and(jnp.max(fast[3]) < GLA_SAFE_GATE_ABS, key_bound[l] < GLA_SAFE_KEY_ABS)
        xp, sg_p, sr_p = lax.cond(bounded, lambda x, kept: kept, lambda x, kept: mixer(x, robust=True)[:3],
                                  xp, fast[:3])
        xp, xs = _ffn(xp, xs, l, norm_g, w2_gu_b, w2_dn_b, 4)
        gla_p.append(sg_p)
        ret_p.append(sr_p)
        gla_s.append(sg_s)
        ret_s.append(sr_s)
    yp, ys = xp, xs

    return (yp.reshape(batch, seq, D_MODEL), ys.reshape(n_seq, n_tok, D_MODEL),
            jnp.stack(gla_p), jnp.stack(ret_p), jnp.stack(gla_s), jnp.stack(ret_s))
```

```python
import functools
import math

import jax
import jax.numpy as jnp
from jax import lax
from jax.experimental import pallas as pl
from jax.experimental.pallas import tpu as pltpu

F32 = jnp.float32
BF16 = jnp.bfloat16

D_MODEL = 1024
DEPTH = 4
PAST_LEN = 16384
N_HEADS = 4
DK_GLA = 64
DV = 128
GATE_RANK = 16
GATE_TEMP = 16.0
D_FF = 2816
ROPE_BASE = 10000.0
EPS = 1e-6
GLA_CHUNK = 64

A_COLS = 1536
R_COLS = 2048
QA, KA, VA, GA = 0, 256, 512, 1024
QR, KR, VR, GR = 0, 512, 1024, 1536
AL_PAD = 128

FFN_SUB_TILE = 512
FFN_FF_TILE = 256
MIX_SUB_TILE = 256
MIX_SUB_TILES_PER_STEP = 1
SAMPLE_SEQ_BLOCK = 16
VMEM_LIMIT_BYTES = 56 * 1024 * 1024
GLA_SAFE_GATE_ABS = 14.0
GLA_SAFE_KEY_ABS = 1e9

LOG_GAMMA = tuple(math.log1p(-(2.0 ** (-5.0 - h))) for h in range(N_HEADS))


def _mm(a, b):
    return jnp.dot(a, b, preferred_element_type=F32)


def _mm_nt(a, b):
    return lax.dot_general(a, b, (((1,), (1,)), ((), ())), preferred_element_type=F32)


def _mm_tn(a, b):
    return lax.dot_general(a, b, (((0,), (0,)), ((), ())), preferred_element_type=F32)


def _rms(x, g):
    ms = jnp.mean(x * x, axis=-1, keepdims=True)
    return x * lax.rsqrt(ms + EPS) * g


def _silu(x):
    return x * (1.0 / (1.0 + jnp.exp(-x)))


def _log_sigmoid(x):
    return jnp.minimum(x, 0.0) - jnp.log1p(jnp.exp(-jnp.abs(x)))


def _split_bf16(x):
    hi = x.astype(BF16)
    lo = (x - hi.astype(F32)).astype(BF16)
    return hi, lo


def _iota(shape, dim):
    return lax.broadcasted_iota(jnp.int32, shape, dim)


def _rotary(x, cos2, sin2):
    return x * cos2 + pltpu.roll(x, DV // 2, axis=1) * sin2


def _head_norm_gate(o, gn, gate):
    outs = []
    for h in range(N_HEADS):
        oh = o[:, h * DV:(h + 1) * DV]
        ms = jnp.mean(oh * oh, axis=-1, keepdims=True)
        outs.append(oh * lax.rsqrt(ms + EPS))
    return jnp.concatenate(outs, axis=1) * gn * _silu(gate)


def _first_of_block(x, block, row):
    bit = 1
    while bit < block:
        x = jnp.where(jnp.bitwise_and(row, bit) != 0, pltpu.roll(x, bit, axis=0), x)
        bit *= 2
    return x


def _stack_heads(x, k_head_mask_b):
    return jnp.concatenate([x.astype(BF16)] * N_HEADS, axis=0) * k_head_mask_b


def _gla_level_operands(qv, kv, bv, lgv, k_head_mask_b):
    n = qv.shape[0]
    row = _iota((n, 256), 0)
    pos16, pos4, pos1 = jnp.right_shift(row, 4), jnp.bitwise_and(jnp.right_shift(row, 2), 3), jnp.bitwise_and(row, 3)
    b_first16 = jnp.concatenate([jnp.broadcast_to(bv[16 * g:16 * g + 1, :], (16, 256))
                                 for g in range(n // 16)], axis=0)
    q1 = qv * jnp.exp(bv - b_first16)
    q2 = qv * jnp.exp(bv - _first_of_block(bv, 4, row))
    k1 = [jnp.where(row < 16 * j, kv * jnp.exp(bv[16 * j:16 * j + 1, :] - bv), 0.0) for j in range(1, 4)]
    k2 = []
    for j in range(1, 4):
        ref = jnp.concatenate([jnp.broadcast_to(bv[16 * g + 4 * j:16 * g + 4 * j + 1, :], (16, 256))
                               for g in range(n // 16)], axis=0)
        k2.append(jnp.where(jnp.bitwise_and(row, 15) < 4 * j, kv * jnp.exp(ref - bv), 0.0))
    f1 = pltpu.roll(lgv, n - 1, axis=0)
    f2 = f1 + pltpu.roll(lgv, n - 2, axis=0)
    f3 = f2 + pltpu.roll(lgv, n - 3, axis=0)
    ahead = [kv, kv * jnp.exp(f1), kv * jnp.exp(f2), kv * jnp.exp(f3)]
    k3 = []
    for j in range(4):
        k_j = jnp.zeros_like(kv)
        for p in range(j + 1):
            k_j = jnp.where(pos1 == p, ahead[j - p], k_j)
        k3.append(k_j)
    lhs = []
    for q_lvl, pos, first_slot in ((q1, pos16, 1), (q2, pos4, 1), (qv, pos1, 0)):
        q_b = q_lvl.astype(BF16)
        lhs += [q_b * jnp.where(pos == j, 1.0, 0.0).astype(BF16) for j in range(first_slot, 4)]
    rhs = [_stack_heads(k_j, k_head_mask_b) for k_j in k1 + k2 + k3]
    return jnp.concatenate(lhs, axis=1), jnp.concatenate(rhs, axis=1)


def _gla_level_scores(lhs, rhs):
    n, w = lhs.shape[0], N_HEADS * DK_GLA
    s1 = _mm_nt(lhs[:, 0:3 * w], rhs[:, 0:3 * w])
    s2 = _mm_nt(lhs[:, 3 * w:6 * w], rhs[:, 3 * w:6 * w])
    s3 = _mm_nt(lhs[:, 6 * w:10 * w], rhs[:, 6 * w:10 * w])
    t_idx = _iota((n, N_HEADS * n), 0)
    s_idx = jnp.bitwise_and(_iota((n, N_HEADS * n), 1), n - 1)
    same16 = jnp.right_shift(t_idx, 4) == jnp.right_shift(s_idx, 4)
    same4 = jnp.right_shift(t_idx, 2) == jnp.right_shift(s_idx, 2)
    return jnp.where(same4, s3, jnp.where(same16, s2, s1))


def _layer_spec(arr, layer, n_grid_axes):
    zeros = (0,) * (arr.ndim - 1)
    if n_grid_axes == 1:
        return pl.BlockSpec((None,) + arr.shape[1:], lambda i: (layer,) + zeros)
    return pl.BlockSpec((None,) + arr.shape[1:], lambda i, j: (layer,) + zeros)


def _params(n_axes):
    return pltpu.CompilerParams(dimension_semantics=("arbitrary",) * n_axes,
                                vmem_limit_bytes=VMEM_LIMIT_BYTES)


def _ffn_rows(x, ng_ref, wgu_ref, wdn_ref, pre_row):
    xn = _rms(x, ng_ref[pre_row:pre_row + 1, :]).astype(BF16)
    acc = jnp.zeros(x.shape, F32)
    for c in range(D_FF // FFN_FF_TILE):
        lo = c * FFN_FF_TILE
        gate = _mm(xn, wgu_ref[:, lo:lo + FFN_FF_TILE])
        up = _mm(xn, wgu_ref[:, D_FF + lo:D_FF + lo + FFN_FF_TILE])
        act = (_silu(gate) * up).astype(BF16)
        acc = acc + _mm(act, wdn_ref[lo:lo + FFN_FF_TILE, :])
    return x + 0.5 * _rms(acc, ng_ref[pre_row + 1:pre_row + 2, :])


def _ffn_kernel(xp_ref, xs_ref, ng_ref, wgu_ref, wdn_ref, yp_ref, ys_ref, *, pre_row, n_prompt_tiles):
    is_prompt = pl.program_id(0) < n_prompt_tiles
    x = jnp.where(is_prompt, xp_ref[...], xs_ref[...])
    y = _ffn_rows(x, ng_ref, wgu_ref, wdn_ref, pre_row)

    @pl.when(is_prompt)
    def _():
        yp_ref[...] = y

    @pl.when(jnp.logical_not(is_prompt))
    def _():
        ys_ref[...] = y


def _ffn(xp, xs, layer, norm_g, wgu, wdn, pre_row):
    tm = FFN_SUB_TILE
    n_p, n_s = xp.shape[0] // tm, xs.shape[0] // tm
    specs = [pl.BlockSpec((tm, D_MODEL), lambda i: (jnp.minimum(i, n_p - 1), 0)),
             pl.BlockSpec((tm, D_MODEL), lambda i: (jnp.maximum(i - n_p, 0), 0))]
    weights = (norm_g, wgu, wdn)
    return pl.pallas_call(
        functools.partial(_ffn_kernel, pre_row=pre_row, n_prompt_tiles=n_p),
        out_shape=(jax.ShapeDtypeStruct(xp.shape, F32), jax.ShapeDtypeStruct(xs.shape, F32)),
        grid=(n_p + n_s,),
        in_specs=specs + [_layer_spec(w, layer, 1) for w in weights],
        out_specs=specs,
        compiler_params=_params(1),
        name="ffn",
    )(xp, xs, *weights)


def _mixer_rows(x, cos2, sin2, sgt, sr, consts, robust, ng_ref, wa_ref, wr_ref, wl_ref, wa2_ref,
                ba_ref, gn_ref, wout_ref, gq_ref, gk_ref):
    tri, k_head_mask, v_head_mask, causal, bd_mask, causal_r = consts
    tt = x.shape[0]
    xn = _rms(x, ng_ref[2:3, :]).astype(BF16)

    alo = _mm(xn, wl_ref[...]).astype(BF16)
    gate_pre = _mm(alo, wa2_ref[...])
    q = _mm(xn, wa_ref[:, QA:QA + 256]) * (DK_GLA ** -0.5)
    k = _mm(xn, wa_ref[:, KA:KA + 256])
    v = _mm(xn, wa_ref[:, VA:VA + 512]).astype(BF16)
    gate_pre = gate_pre + ba_ref[...]
    gate_abs = jnp.max(jnp.abs(gate_pre), axis=0, keepdims=True)
    gate_range = jnp.broadcast_to(jnp.maximum(gate_abs[:, 0:128], gate_abs[:, 128:256]), (8, 128))
    lg = _log_sigmoid(gate_pre) * (1.0 / GATE_TEMP)
    lg_hi, lg_lo = _split_bf16(lg)
    bcum = _mm(tri, lg_hi) + _mm(tri, lg_lo)
    qr = _mm(xn, wr_ref[:, QR:QR + 512])
    kr = _mm(xn, wr_ref[:, KR:KR + 512])
    b_end = bcum[tt - 1:tt, :]

    o_inter = _mm_nt((q * jnp.exp(bcum)).astype(BF16), sgt.astype(BF16))
    k_end = (k * jnp.exp(b_end - bcum)).astype(BF16)
    sgt = jnp.exp(b_end) * sgt + bd_mask * _mm_tn(v, k_end)
    vr = _mm(xn, wr_ref[:, VR:VR + 512]).astype(BF16)

    n_chunks = tt // GLA_CHUNK
    chunk = lambda j: slice(j * GLA_CHUNK, (j + 1) * GLA_CHUNK)
    chunk_last = lambda j: bcum[(j + 1) * GLA_CHUNK - 1:(j + 1) * GLA_CHUNK, :]
    vstacks = [jnp.where(v_head_mask, jnp.concatenate([v[chunk(j)]] * N_HEADS, axis=0), 0.0)
               for j in range(n_chunks)]

    def gla_cross_scores(cj):
        later = slice((cj + 1) * GLA_CHUNK, tt)
        q_later = (q[later] * jnp.exp(bcum[later] - chunk_last(cj))).astype(BF16)
        kc = k[chunk(cj)] * jnp.exp(chunk_last(cj) - bcum[chunk(cj)])
        return _mm_nt(q_later, _stack_heads(kc, k_head_mask))

    def gla_cross_values(cj, scores):
        return _mm(scores.astype(BF16), vstacks[cj])

    def gla_same_scores(ci):
        if robust:
            return _gla_level_scores(*_gla_level_operands(q[chunk(ci)], k[chunk(ci)], bcum[chunk(ci)],
                                                          lg[chunk(ci)], k_head_mask))
        ref = chunk_last(ci - 1) if ci > 0 else jnp.zeros((1, 256), F32)
        qc = (q[chunk(ci)] * jnp.exp(bcum[chunk(ci)] - ref)).astype(BF16)
        kc = k[chunk(ci)] * jnp.exp(ref - bcum[chunk(ci)])
        return jnp.where(causal, _mm_nt(qc, _stack_heads(kc, k_head_mask)), 0.0)

    def gla_same_values(ci, scores):
        return _mm(scores.astype(BF16), vstacks[ci])

    def ret_scores(h):
        hs = slice(h * DV, (h + 1) * DV)
        qh = (_rotary(qr[:, hs], cos2, sin2) * gq_ref[h]).astype(BF16)
        kh = (_rotary(kr[:, hs], cos2, sin2) * gk_ref[h]).astype(BF16)
        vh = vr[:, hs]
        a = _mm_nt(qh, kh)
        o_state = _mm(qh, sr[h].astype(BF16))
        s_new = math.exp(tt * LOG_GAMMA[h]) * (sr[h] + _mm_tn(kh, vh))
        return a, o_state, s_new, vh

    def ret_values(h, staged):
        a, o_state, _, vh = staged
        return o_state + _mm(jnp.where(causal_r, a, 0.0).astype(BF16), vh)

    stage1 = {"ret": ret_scores, "same": gla_same_scores, "cross": gla_cross_scores}
    stage2 = {"ret": ret_values, "same": gla_same_values, "cross": gla_cross_values}
    items = [("cross", i) for i in range(n_chunks - 1)]
    for i in range(max(n_chunks, N_HEADS)):
        if i < n_chunks:
            items.append(("same", i))
        if i < N_HEADS:
            items.append(("ret", i))
    gate_cols = [(wa_ref, GA), (wa_ref, GA + 256), (wr_ref, GR), (wr_ref, GR + 256)]
    gate_parts, staged, done = [], {}, {}
    depth = 1
    for step in range(len(items) + depth):
        if step < len(items):
            kind, i = items[step]
            staged[(kind, i)] = stage1[kind](i)
        if step % 2 == 1 and len(gate_parts) < len(gate_cols):
            w_ref, off = gate_cols[len(gate_parts)]
            gate_parts.append(_mm(xn, w_ref[:, off:off + 256]))
        if step >= depth:
            kind, i = items[step - depth]
            done[(kind, i)] = stage2[kind](i, staged[(kind, i)])
    while len(gate_parts) < len(gate_cols):
        w_ref, off = gate_cols[len(gate_parts)]
        gate_parts.append(_mm(xn, w_ref[:, off:off + 256]))
    gate_a = jnp.concatenate(gate_parts[0:2], axis=1)
    gate_r = jnp.concatenate(gate_parts[2:4], axis=1)

    o_a = o_inter + jnp.concatenate([done[("same", ci)] for ci in range(n_chunks)], axis=0)
    for cj in range(n_chunks - 1):
        o_a = o_a + jnp.concatenate(
            [jnp.zeros(((cj + 1) * GLA_CHUNK, N_HEADS * DV), F32), done[("cross", cj)]], axis=0)
    o_r = jnp.concatenate([done[("ret", h)] for h in range(N_HEADS)], axis=1)
    sr_new = [staged[("ret", h)][2] for h in range(N_HEADS)]

    y_a = _head_norm_gate(o_a, gn_ref[:, 0:512], gate_a)
    y_r = _head_norm_gate(o_r, gn_ref[:, 512:1024], gate_r)
    y = jnp.concatenate([y_a, y_r], axis=1).astype(BF16)
    return x + _rms(_mm(y, wout_ref[...]), ng_ref[3:4, :]), sgt, sr_new, gate_range


def _mixer_prompt_kernel(x_ref, ng_ref, wa_ref, wr_ref, wl_ref, wa2_ref, ba_ref, gn_ref, wout_ref,
                         cos_ref, sin_ref, gq_ref, gk_ref, o_ref, sg_ref, sr_ref, range_ref, sgt_s, sr_s, *, robust):
    t_idx = pl.program_id(1)

    @pl.when(t_idx == 0)
    def _():
        sgt_s[...] = jnp.zeros(sgt_s.shape, F32)
        sr_s[...] = jnp.zeros(sr_s.shape, F32)

    sub = MIX_SUB_TILE
    rows4 = N_HEADS * GLA_CHUNK
    r = _iota((sub, sub), 0)
    c = _iota((sub, sub), 1)
    consts = (
        jnp.where(c <= r, 1.0, 0.0).astype(BF16),
        jnp.where(jnp.right_shift(_iota((rows4, 256), 0), 6)
                  == jnp.right_shift(_iota((rows4, 256), 1), 6), 1.0, 0.0).astype(BF16),
        jnp.right_shift(_iota((rows4, 512), 0), 6) == jnp.right_shift(_iota((rows4, 512), 1), 7),
        jnp.bitwise_and(_iota((GLA_CHUNK, rows4), 1), GLA_CHUNK - 1) <= _iota((GLA_CHUNK, rows4), 0),
        jnp.where(jnp.right_shift(_iota((512, 256), 0), 7)
                  == jnp.right_shift(_iota((512, 256), 1), 6), 1.0, 0.0),
        c <= r,
    )
    sgt = sgt_s[...]
    sr = [sr_s[h] for h in range(N_HEADS)]
    for s in range(x_ref.shape[0] // sub):
        rs = slice(s * sub, (s + 1) * sub)
        y, sgt, sr, gate_range = _mixer_rows(x_ref[rs, :], cos_ref[rs, :], sin_ref[rs, :], sgt, sr, consts,
                                             robust, ng_ref, wa_ref, wr_ref, wl_ref, wa2_ref, ba_ref, gn_ref,
                                             wout_ref, gq_ref, gk_ref)
        o_ref[rs, :] = y
        first = jnp.logical_and(pl.program_id(0) == 0, jnp.logical_and(t_idx == 0, s == 0))
        range_ref[...] = jnp.where(first, gate_range, jnp.maximum(range_ref[...], gate_range))
    sgt_s[...] = sgt
    for h in range(N_HEADS):
        sr_s[h] = sr[h]

    @pl.when(t_idx == pl.num_programs(1) - 1)
    def _():
        sg = sgt_s[...].T
        for h in range(N_HEADS):
            sg_ref[0, h] = sg[h * DK_GLA:(h + 1) * DK_GLA, h * DV:(h + 1) * DV]
        sr_ref[0] = sr_s[...]


def _mixer_prompt(x_prompt, robust, batch, seq, tt, layer, norm_g, wa, wr, wl, wa2, ba, gn, wout, cos2, sin2,
                  gq, gk):
    n_t = seq // tt
    x_map = lambda b, t: (b * n_t + t, 0)
    const3 = lambda b, t: (0, 0, 0)
    out_shapes = (
        jax.ShapeDtypeStruct((batch * seq, D_MODEL), F32),
        jax.ShapeDtypeStruct((batch, N_HEADS, DK_GLA, DV), F32),
        jax.ShapeDtypeStruct((batch, N_HEADS, DV, DV), F32),
        jax.ShapeDtypeStruct((8, 128), F32),
    )
    weights = (norm_g, wa, wr, wl, wa2, ba, gn, wout)
    return pl.pallas_call(
        functools.partial(_mixer_prompt_kernel, robust=robust),
        out_shape=out_shapes,
        grid=(batch, n_t),
        in_specs=[pl.BlockSpec((tt, D_MODEL), x_map)]
        + [_layer_spec(w, layer, 2) for w in weights]
        + [pl.BlockSpec((tt, DV), lambda b, t: (t, 0)),
           pl.BlockSpec((tt, DV), lambda b, t: (t, 0)),
           pl.BlockSpec(gq.shape, const3),
           pl.BlockSpec(gk.shape, const3)],
        out_specs=(
            pl.BlockSpec((tt, D_MODEL), x_map),
            pl.BlockSpec((1, N_HEADS, DK_GLA, DV), lambda b, t: (b, 0, 0, 0)),
            pl.BlockSpec((1, N_HEADS, DV, DV), lambda b, t: (b, 0, 0, 0)),
            pl.BlockSpec((8, 128), lambda b, t: (0, 0)),
        ),
        scratch_shapes=[
            pltpu.VMEM((N_HEADS * DV, N_HEADS * DK_GLA), F32),
            pltpu.VMEM((N_HEADS, DV, DV), F32),
        ],
        compiler_params=_params(2),
        name="mixer_prompt_levels" if robust else "mixer_prompt",
    )(x_prompt, *weights, cos2, sin2, gq, gk)


def _proj_sample_kernel(x_ref, ng_ref, wa_ref, wr_ref, wl_ref, wa2_ref, ba_ref, pa_ref, pr_ref, lg_ref):
    xn = _rms(x_ref[...], ng_ref[2:3, :]).astype(BF16)
    pa_ref[...] = _mm(xn, wa_ref[...])
    pr_ref[...] = _mm(xn, wr_ref[...])
    alo = _mm(xn, wl_ref[...]).astype(BF16)
    lg_ref[...] = _log_sigmoid(_mm(alo, wa2_ref[...]) + ba_ref[...]) * (1.0 / GATE_TEMP)


def _proj_sample(x_sample, layer, norm_g, wa, wr, wl, wa2, ba):
    n_rows = x_sample.shape[0]
    const = lambda i: (0, 0)
    weights = (norm_g, wa, wr, wl, wa2, ba)
    return pl.pallas_call(
        _proj_sample_kernel,
        out_shape=(jax.ShapeDtypeStruct((n_rows, A_COLS), F32),
                   jax.ShapeDtypeStruct((n_rows, R_COLS), F32),
                   jax.ShapeDtypeStruct((n_rows, 256), F32)),
        grid=(1,),
        in_specs=[pl.BlockSpec((n_rows, D_MODEL), const)] + [_layer_spec(w, layer, 1) for w in weights],
        out_specs=(pl.BlockSpec((n_rows, A_COLS), const),
                   pl.BlockSpec((n_rows, R_COLS), const),
                   pl.BlockSpec((n_rows, 256), const)),
        compiler_params=_params(1),
        name="proj_sample",
    )(x_sample, *weights)


def _core_sample_kernel(pa_ref, pr_ref, lg_ref, cos_ref, sin_ref, gq_ref, gk_ref, sg_in, sr_in,
                        o_ref, sg_out, sr_out, *, n_tok):
    n_seq = sg_in.shape[0]
    rows4 = N_HEADS * n_tok
    tok_shift = n_tok.bit_length() - 1
    k_head_mask = jnp.where(jnp.right_shift(_iota((rows4, 256), 0), tok_shift)
                            == jnp.right_shift(_iota((rows4, 256), 1), 6), 1.0, 0.0).astype(BF16)
    v_head_mask = (jnp.right_shift(_iota((rows4, 512), 0), tok_shift)
                   == jnp.right_shift(_iota((rows4, 512), 1), 7))
    tok_row = _iota((n_tok, 256), 0)
    causal_r = _iota((n_tok, n_tok), 1) <= _iota((n_tok, n_tok), 0)
    ones = jnp.ones((2 * n_tok, DV), BF16)
    cos2 = cos_ref[...]
    sin2 = sin_ref[...]

    def one_seq(n, pa, pr, lg):
        b_rows = [lg[0:1]]
        for t in range(1, n_tok):
            b_rows.append(b_rows[-1] + lg[t:t + 1])
        bc = jnp.concatenate(b_rows, axis=0)
        b_last = b_rows[-1]
        k = pa[:, KA:KA + 256]
        v = pa[:, VA:VA + 512].astype(BF16)
        q = pa[:, QA:QA + 256] * (DK_GLA ** -0.5)
        qt = (q * jnp.exp(bc)).astype(BF16)
        kp = (k * jnp.exp(b_last - bc)).astype(BF16)
        s_old = sg_in[n].reshape(N_HEADS * DK_GLA, DV)
        s_old_b = s_old.astype(BF16)
        s_bd = jnp.concatenate(
            [jnp.where(jnp.right_shift(_iota((N_HEADS * DK_GLA, DV), 0), 6) == h, s_old_b, 0.0)
             for h in range(N_HEADS)], axis=1)
        vstack = jnp.where(v_head_mask, jnp.concatenate([v] * N_HEADS, axis=0), 0.0)
        lhs = [jnp.where(tok_row == j, q, 0.0).astype(BF16) for j in range(n_tok)]
        rhs = [_stack_heads(jnp.where(tok_row <= j, k * jnp.exp(b_rows[j] - bc), 0.0), k_head_mask)
               for j in range(n_tok)]
        a = _mm_nt(jnp.concatenate(lhs, axis=1), jnp.concatenate(rhs, axis=1)).astype(BF16)
        o_a = _mm(qt, s_bd) + _mm(a, vstack)
        lg_hi, lg_lo = _split_bf16(lg)
        b_last_t = _mm_tn(jnp.concatenate([lg_hi, lg_lo], axis=0), ones)
        u = _mm_tn(kp, v)
        u_d = jnp.concatenate(
            [u[h * DK_GLA:(h + 1) * DK_GLA, h * DV:(h + 1) * DV] for h in range(N_HEADS)], axis=0)
        sg_out[n] = (jnp.exp(b_last_t) * s_old + u_d).reshape(N_HEADS, DK_GLA, DV)
        vr = pr[:, VR:VR + 512].astype(BF16)
        o_r_heads = []
        for h in range(N_HEADS):
            hs = slice(h * DV, (h + 1) * DV)
            qh = (_rotary(pr[:, QR + h * DV:QR + (h + 1) * DV], cos2, sin2) * gq_ref[h]).astype(BF16)
            kh = (_rotary(pr[:, KR + h * DV:KR + (h + 1) * DV], cos2, sin2) * gk_ref[h]).astype(BF16)
            vh = vr[:, hs]
            ar = jnp.where(causal_r, _mm_nt(qh, kh), 0.0).astype(BF16)
            s_r = sr_in[n, h]
            o_r_heads.append(_mm(qh, s_r.astype(BF16)) + _mm(ar, vh))
            sr_out[n, h] = math.exp(n_tok * LOG_GAMMA[h]) * (s_r + _mm_tn(kh, vh))
        return jnp.concatenate([o_a] + o_r_heads, axis=1)

    seq_per_iter = 8 // n_tok

    def body(i, carry):
        rows = pl.ds(pl.multiple_of(i * 8, 8), 8)
        pa8, pr8, lg8 = pa_ref[rows, :], pr_ref[rows, :], lg_ref[rows, :]
        outs = []
        for j in range(seq_per_iter):
            sl = slice(j * n_tok, (j + 1) * n_tok)
            outs.append(one_seq(i * seq_per_iter + j, pa8[sl], pr8[sl], lg8[sl]))
        o_ref[rows, :] = jnp.concatenate(outs, axis=0)
        return carry

    lax.fori_loop(0, n_seq // seq_per_iter, body, 0, unroll=True)


def _core_sample(pa, pr, lg, layer, n_tok, cos2, sin2, gq, gk, state_gla, state_ret):
    n_seq = state_gla.shape[1]
    nb = SAMPLE_SEQ_BLOCK
    rows = nb * n_tok
    const2 = lambda i: (0, 0)
    const3 = lambda i: (0, 0, 0)
    out_shapes = (
        jax.ShapeDtypeStruct((n_seq * n_tok, D_MODEL), F32),
        jax.ShapeDtypeStruct(state_gla.shape[1:], F32),
        jax.ShapeDtypeStruct(state_ret.shape[1:], F32),
    )
    return pl.pallas_call(
        functools.partial(_core_sample_kernel, n_tok=n_tok),
        out_shape=out_shapes,
        grid=(n_seq // nb,),
        in_specs=[
            pl.BlockSpec((rows, A_COLS), lambda i: (i, 0)),
            pl.BlockSpec((rows, R_COLS), lambda i: (i, 0)),
            pl.BlockSpec((rows, 256), lambda i: (i, 0)),
            pl.BlockSpec(cos2.shape, const2),
            pl.BlockSpec(sin2.shape, const2),
            pl.BlockSpec(gq.shape, const3),
            pl.BlockSpec(gk.shape, const3),
            pl.BlockSpec((None, nb, N_HEADS, DK_GLA, DV), lambda i: (layer, i, 0, 0, 0)),
            pl.BlockSpec((None, nb, N_HEADS, DV, DV), lambda i: (layer, i, 0, 0, 0)),
        ],
        out_specs=(
            pl.BlockSpec((rows, D_MODEL), lambda i: (i, 0)),
            pl.BlockSpec((nb, N_HEADS, DK_GLA, DV), lambda i: (i, 0, 0, 0)),
            pl.BlockSpec((nb, N_HEADS, DV, DV), lambda i: (i, 0, 0, 0)),
        ),
        compiler_params=_params(1),
        name="core_sample",
    )(pa, pr, lg, cos2, sin2, gq, gk, state_gla, state_ret)


def _out_sample_kernel(o_ref, ga_ref, gr_ref, gn_ref, wout_ref, ng_ref, x_ref, y_ref):
    o = o_ref[...]
    y_a = _head_norm_gate(o[:, 0:512], gn_ref[:, 0:512], ga_ref[...])
    y_r = _head_norm_gate(o[:, 512:1024], gn_ref[:, 512:1024], gr_ref[...])
    y = jnp.concatenate([y_a, y_r], axis=1).astype(BF16)
    y_ref[...] = x_ref[...] + _rms(_mm(y, wout_ref[...]), ng_ref[3:4, :])


def _out_sample(o2, pa, pr, layer, gn, wout, norm_g, x_sample):
    n_rows = o2.shape[0]
    const = lambda i: (0, 0)
    return pl.pallas_call(
        _out_sample_kernel,
        out_shape=jax.ShapeDtypeStruct((n_rows, D_MODEL), F32),
        grid=(1,),
        in_specs=[
            pl.BlockSpec(o2.shape, const),
            pl.BlockSpec((n_rows, 512), lambda i: (0, GA // 512)),
            pl.BlockSpec((n_rows, 512), lambda i: (0, GR // 512)),
            _layer_spec(gn, layer, 1),
            _layer_spec(wout, layer, 1),
            _layer_spec(norm_g, layer, 1),
            pl.BlockSpec((n_rows, D_MODEL), const),
        ],
        out_specs=pl.BlockSpec((n_rows, D_MODEL), const),
        compiler_params=_params(1),
        name="out_sample",
    )(o2, pa, pr, gn, wout, norm_g, x_sample)


def _rope_tables(pos):
    half = DV // 2
    inv_freq = ROPE_BASE ** (-jnp.arange(half, dtype=F32) / half)
    ang = pos[:, None] * inv_freq[None, :]
    cos, sin = jnp.cos(ang), jnp.sin(ang)
    return jnp.concatenate([cos, cos], axis=-1), jnp.concatenate([-sin, sin], axis=-1)


def _decay_tables(chunk):
    steps = (jnp.arange(chunk, dtype=F32) + 1.0)[None, :, None]
    lgam = jnp.asarray(LOG_GAMMA, F32)[:, None, None]
    gq = jnp.broadcast_to(jnp.exp(steps * lgam), (N_HEADS, chunk, DV))
    gk = jnp.broadcast_to(jnp.exp(-steps * lgam) * (DV ** -0.5), (N_HEADS, chunk, DV))
    return gq, gk


def kernel(x_prompt, x_sample, state_gla, state_ret, norm_g, w1_gu, w1_down, w2_gu, w2_down,
           w_in, w_a2, b_a, gn_gla, gn_ret, w_out):
    batch, seq, _ = x_prompt.shape
    n_seq, n_tok, _ = x_sample.shape
    n_prompt = batch * seq
    n_sample = n_seq * n_tok
    assert n_prompt % FFN_SUB_TILE == 0 and n_sample % FFN_SUB_TILE == 0
    assert seq % (MIX_SUB_TILE * MIX_SUB_TILES_PER_STEP) == 0
    assert n_seq % SAMPLE_SEQ_BLOCK == 0 and n_tok in (1, 2, 4, 8)
    mix_tile = MIX_SUB_TILE * MIX_SUB_TILES_PER_STEP

    w1_gu_b, w1_dn_b = w1_gu.astype(BF16), w1_down.astype(BF16)
    w2_gu_b, w2_dn_b = w2_gu.astype(BF16), w2_down.astype(BF16)
    w_in_b = w_in.astype(BF16)
    wa_b = w_in_b[..., :A_COLS]
    wr_b = w_in_b[..., A_COLS + GATE_RANK:]
    wl_b = jnp.pad(w_in_b[..., A_COLS:A_COLS + GATE_RANK], ((0, 0), (0, 0), (0, AL_PAD - GATE_RANK)))
    wa2_b = jnp.pad(w_a2.astype(BF16), ((0, 0), (0, AL_PAD - GATE_RANK), (0, 0)))
    w_out_b = w_out.astype(BF16)
    b_a3 = b_a[:, None, :]
    gn = jnp.concatenate([gn_gla, gn_ret], axis=-1)[:, None, :]

    key_bound = (math.sqrt(D_MODEL) * jnp.max(jnp.abs(norm_g[:, 2, :]), axis=-1)
                 * jnp.max(jnp.sqrt(jnp.sum(jnp.square(w_in[:, :, KA:KA + 256]), axis=1)), axis=-1))

    cos_p, sin_p = _rope_tables(jnp.arange(seq, dtype=F32))
    cos_s, sin_s = _rope_tables(PAST_LEN + jnp.arange(n_tok, dtype=F32))
    gq_p, gk_p = _decay_tables(MIX_SUB_TILE)
    gq_s, gk_s = _decay_tables(n_tok)

    xp = x_prompt.reshape(n_prompt, D_MODEL)
    xs = x_sample.reshape(n_sample, D_MODEL)
    gla_p, ret_p, gla_s, ret_s = [], [], [], []
    for l in range(DEPTH):
        xp, xs = _ffn(xp, xs, l, norm_g, w1_gu_b, w1_dn_b, 0)
        pa, pr, lg = _proj_sample(xs, l, norm_g, wa_b, wr_b, wl_b, wa2_b, b_a3)
        o2, sg_s, sr_s = _core_sample(pa, pr, lg, l, n_tok, cos_s, sin_s, gq_s, gk_s, state_gla, state_ret)
        xs = _out_sample(o2, pa, pr, l, gn, w_out_b, norm_g, xs)
        mixer = functools.partial(_mixer_prompt, batch=batch, seq=seq, tt=mix_tile, layer=l, norm_g=norm_g,
                                  wa=wa_b, wr=wr_b, wl=wl_b, wa2=wa2_b, ba=b_a3, gn=gn, wout=w_out_b,
                                  cos2=cos_p, sin2=sin_p, gq=gq_p, gk=gk_p)
        fast = mixer(xp, robust=False)
        bounded = jnp.logical_and(jnp.max(fast[3]) < GLA_SAFE_GATE_ABS, key_bound[l] < GLA_SAFE_KEY_ABS)
        xp, sg_p, sr_p = lax.cond(bounded, lambda x, kept: kept, lambda x, kept: mixer(x, robust=True)[:3],
                                  xp, fast[:3])
        xp, xs = _ffn(xp, xs, l, norm_g, w2_gu_b, w2_dn_b, 4)
        gla_p.append(sg_p)
        ret_p.append(sr_p)
        gla_s.append(sg_s)
        ret_s.append(sr_s)
    yp, ys = xp, xs

    return (yp.reshape(batch, seq, D_MODEL), ys.reshape(n_seq, n_tok, D_MODEL),
            jnp.stack(gla_p), jnp.stack(ret_p), jnp.stack(gla_s), jnp.stack(ret_s))
```
